```python
import math
import jax, jax.numpy as jnp
from jax import lax
import numpy as np


D_MODEL = 1024
BATCH = 2
SEQ = 16384
DEPTH = 4

GRID_W = 64
CTX_LEN = 256
N_MIXERS = 4
Q_BLOCK = 128
ROPE_BASE = 10000.0
EPS = 1e-6
ADA_CHUNKS = 6

RET_HEADS = 4
RET_DK = D_MODEL // RET_HEADS
RET_DV = 2 * RET_DK
RET_CHUNK = 128

DIFF_HEADS = 8
DIFF_DH = D_MODEL // (2 * DIFF_HEADS)
DIFF_DV = 2 * DIFF_DH

NA_HEADS = 16
NA_DH = D_MODEL // NA_HEADS
NA_WIN_H = 8
NA_WIN_W = 16

MLA_HEADS = 16
MLA_Q_RANK = 256
MLA_KV_RANK = 128
MLA_NOPE = 64
MLA_ROPE = 32
MLA_V = 64
MLA_QK = MLA_NOPE + MLA_ROPE

FFN_HIDDEN = -(-8 * D_MODEL // (3 * 256)) * 256

kernel_name = 'hybrid_interleaved_diffusion_trunk'


def rms_norm(x):
    xf = x.astype(jnp.float32)
    return (xf * lax.rsqrt(jnp.mean(xf * xf, axis=-1, keepdims=True) + EPS)).astype(x.dtype)


def rms_norm_gain(x, g):
    return rms_norm(x) * g.astype(x.dtype)


def modulate(x, shift, scale):
    return rms_norm(x) * (1.0 + scale) + shift


def swiglu(h, w_in, w_out):
    a, g = jnp.split(h @ w_in, 2, axis=-1)
    return (jax.nn.silu(a) * g) @ w_out


def rope_2d(x):
    n, d = x.shape[1], x.shape[-1]
    half = d // 2
    quarter = half // 2
    pos = jnp.arange(n)
    row = (pos // GRID_W).astype(jnp.float32)
    col = (pos % GRID_W).astype(jnp.float32)
    inv_freq = jnp.power(ROPE_BASE, -jnp.arange(quarter, dtype=jnp.float32) / quarter)
    bshape = (n,) + (1,) * (x.ndim - 3) + (quarter,)

    def rot(t, p):
        ang = (p[:, None] * inv_freq[None, :]).reshape(bshape)
        cos, sin = jnp.cos(ang).astype(t.dtype), jnp.sin(ang).astype(t.dtype)
        t1, t2 = t[..., :quarter], t[..., quarter:]
        return jnp.concatenate([t1 * cos - t2 * sin, t2 * cos + t1 * sin], axis=-1)

    return jnp.concatenate([rot(x[..., :half], row), rot(x[..., half:], col)], axis=-1)


def map_query_blocks(fn, q):
    b, n = q.shape[0], q.shape[1]
    qb = jnp.moveaxis(q.reshape((b, n // Q_BLOCK, Q_BLOCK) + q.shape[2:]), 1, 0)
    out = jnp.moveaxis(lax.map(fn, qb), 0, 1)
    return out.reshape((b, n) + out.shape[3:])


def softmax_attend(q, k, v, scale):
    s = jnp.einsum('bqhd,bkhd->bhqk', q, k).astype(jnp.float32) * scale
    p = jax.nn.softmax(s, axis=-1).astype(v.dtype)
    return jnp.einsum('bhqk,bkhv->bqhv', p, v)


def chunk_retention(q, k, v, log_gamma, s0):
    b, n, h, _ = q.shape
    dv = v.shape[-1]
    nc = n // RET_CHUNK
    f32 = jnp.float32

    def chunks(t):
        return jnp.moveaxis(t.astype(f32).reshape(b, nc, RET_CHUNK, h, t.shape[-1]), 1, 0)

    idx = jnp.arange(RET_CHUNK, dtype=f32)
    rel = idx[:, None] - idx[None, :]
    d_intra = jnp.where(rel >= 0, jnp.exp(jnp.maximum(rel, 0.0) * log_gamma[:, None, None]), 0.0)
    d_cross = jnp.exp((idx + 1.0)[None, :] * log_gamma[:, None]).T[None, :, :, None]
    d_state = jnp.exp((RET_CHUNK - 1.0 - idx)[None, :] * log_gamma[:, None]).T[None, :, :, None]
    d_chunk = jnp.exp(RET_CHUNK * log_gamma)[None, :, None, None]

    def step(s, qkv):
        qc, kc, vc = qkv
        att = jnp.einsum('bihd,bjhd->bhij', qc, kc) * d_intra
        o = jnp.einsum('bhij,bjhv->bihv', att, vc) + jnp.einsum('bihd,bhdv->bihv', qc, s) * d_cross
        s = s * d_chunk + jnp.einsum('bjhd,bjhv->bhdv', kc * d_state, vc)
        return s, o

    _, o = lax.scan(step, s0.astype(f32), (chunks(q), chunks(k), chunks(v)))
    return jnp.moveaxis(o, 0, 1).reshape(b, n, h, dv)


def retention_output(o, g, norm_g, w_out):
    b, n, h, dv = o.shape
    of = o.astype(jnp.float32)
    mu = jnp.mean(of, axis=-1, keepdims=True)
    var = jnp.mean(jnp.square(of - mu), axis=-1, keepdims=True)
    on = ((of - mu) * lax.rsqrt(var + EPS)).astype(g.dtype).reshape(b, n, h * dv) * norm_g
    return (on * jax.nn.silu(g)) @ w_out


def retention_mixer(h_ctx, h_lat, w_in, decay_logit, norm_g, w_out, need_ctx):
    f32 = jnp.float32
    nqk = RET_HEADS * RET_DK
    splits = [nqk, 2 * nqk, 2 * nqk + RET_HEADS * RET_DV]

    def proj(h):
        b, n, _ = h.shape
        q, k, v, g = jnp.split(h @ w_in, splits, axis=-1)
        return (q.reshape(b, n, RET_HEADS, RET_DK),
                k.reshape(b, n, RET_HEADS, RET_DK) * (RET_DK ** -0.5),
                v.reshape(b, n, RET_HEADS, RET_DV), g)

    qc, kc, vc, gc = proj(h_ctx)
    ql, kl, vl, gl = proj(h_lat)
    ql, kl = rope_2d(ql), rope_2d(kl)
    log_gamma = jax.nn.log_sigmoid(decay_logit.astype(f32))
    lf, lb = log_gamma[0], log_gamma[1]
    n_ctx = h_ctx.shape[1]
    pos = jnp.arange(n_ctx, dtype=f32)
    kcf, vcf = kc.astype(f32), vc.astype(f32)
    w_f = jnp.exp((n_ctx - 1.0 - pos)[:, None] * lf[None, :])[None, :, :, None]
    w_b = jnp.exp(pos[:, None] * lb[None, :])[None, :, :, None]
    s_f = jnp.einsum('blhk,blhv->bhkv', kcf * w_f, vcf)
    s_b = jnp.einsum('blhk,blhv->bhkv', kcf * w_b, vcf)
    o_lat = (chunk_retention(ql, kl, vl, lf, s_f)
             + chunk_retention(ql[:, ::-1], kl[:, ::-1], vl[:, ::-1], lb, s_b)[:, ::-1])
    y_lat = retention_output(o_lat, gl, norm_g, w_out)
    if not need_ctx:
        return None, y_lat
    rel = pos[:, None] - pos[None, :]
    d_bi = (jnp.where(rel >= 0, jnp.exp(jnp.maximum(rel, 0.0) * lf[:, None, None]), 0.0)
            + jnp.where(rel <= 0, jnp.exp(jnp.maximum(-rel, 0.0) * lb[:, None, None]), 0.0))
    att = jnp.einsum('bihd,bjhd->bhij', qc.astype(f32), kcf) * d_bi
    o_ctx = jnp.einsum('bhij,bjhv->bihv', att, vcf)
    return retention_output(o_ctx, gc, norm_g, w_out), y_lat


def diff_attend(q, k, v, lam, scale):
    s = jnp.einsum('bqhmd,bkhmd->bhmqk', q, k).astype(jnp.float32) * scale
    p = jax.nn.softmax(s, axis=-1)
    p = p[:, :, 0] - lam * p[:, :, 1]
    return jnp.einsum('bhqk,bkhv->bqhv', p.astype(v.dtype), v)


def diff_attention_mixer(h_ctx, h_lat, w_in, q_norm_g, k_norm_g, lam_vecs, subln_g, w_out,
                         lambda_init, need_ctx):
    nq = DIFF_HEADS * 2 * DIFF_DH

    def proj(h):
        b, n, _ = h.shape
        q, k, v = jnp.split(h @ w_in, [nq, 2 * nq], axis=-1)
        q = rms_norm_gain(q.reshape(b, n, DIFF_HEADS, 2, DIFF_DH), q_norm_g)
        k = rms_norm_gain(k.reshape(b, n, DIFF_HEADS, 2, DIFF_DH), k_norm_g)
        return q, k, v.reshape(b, n, DIFF_HEADS, DIFF_DV)

    qc, kc, vc = proj(h_ctx)
    ql, kl, vl = proj(h_lat)
    ql, kl = rope_2d(ql), rope_2d(kl)
    lv = lam_vecs.astype(jnp.float32)
    lam = jnp.exp(jnp.sum(lv[0] * lv[1])) - jnp.exp(jnp.sum(lv[2] * lv[3])) + lambda_init
    scale = DIFF_DH ** -0.5

    def out(o):
        b, n = o.shape[0], o.shape[1]
        o = rms_norm_gain(o, subln_g) * (1.0 - lambda_init)
        return o.reshape(b, n, DIFF_HEADS * DIFF_DV) @ w_out

    k_all = jnp.concatenate([kc, kl], axis=1)
    v_all = jnp.concatenate([vc, vl], axis=1)
    y_lat = out(map_query_blocks(lambda qb: diff_attend(qb, k_all, v_all, lam, scale), ql))
    if not need_ctx:
        return None, y_lat
    return out(diff_attend(qc, kc, vc, lam, scale)), y_lat


def neighborhood_mixer(h_ctx, h_lat, w_in, q_norm_g, k_norm_g, rel_bias, w_out, need_ctx):
    def proj(h):
        b, n, _ = h.shape
        q, k, v = jnp.split(h @ w_in, 3, axis=-1)
        q = rms_norm_gain(q.reshape(b, n, NA_HEADS, NA_DH), q_norm_g)
        k = rms_norm_gain(k.reshape(b, n, NA_HEADS, NA_DH), k_norm_g)
        return q, k, v.reshape(b, n, NA_HEADS, NA_DH)

    qc, kc, vc = proj(h_ctx)
    ql, kl, vl = proj(h_lat)
    scale = NA_DH ** -0.5
    b, n = ql.shape[0], ql.shape[1]
    rows = n // GRID_W
    wh = min(NA_WIN_H, rows)
    n_loc = wh * NA_WIN_W

    def grid(t):
        return t.reshape(b, rows, GRID_W, NA_HEADS, NA_DH)

    kg, vg = grid(kl), grid(vl)
    col = np.arange(GRID_W)
    col_start = np.clip(col - NA_WIN_W // 2, 0, GRID_W - NA_WIN_W)
    col_idx = col_start[:, None] + np.arange(NA_WIN_W)[None, :]
    dc_idx = col_idx - col[:, None] + (NA_WIN_W - 1)

    def row_block(args):
        r, q_row = args
        r0 = jnp.clip(r - NA_WIN_H // 2, 0, rows - wh)
        k_win = lax.dynamic_slice_in_dim(kg, r0, wh, axis=1)[:, :, col_idx]
        v_win = lax.dynamic_slice_in_dim(vg, r0, wh, axis=1)[:, :, col_idx]
        dr_idx = r0 + jnp.arange(wh) - r + (NA_WIN_H - 1)
        bias = rel_bias[:, dr_idx[:, None, None], dc_idx[None, :, :]].transpose(0, 2, 1, 3)
        s_loc = (jnp.einsum('bqhd,brqwhd->bhqrw', q_row, k_win).astype(jnp.float32) * scale
                 + bias.astype(jnp.float32))
        s_ctx = jnp.einsum('bqhd,bkhd->bhqk', q_row, kc).astype(jnp.float32) * scale
        s = jnp.concatenate([s_loc.reshape(b, NA_HEADS, GRID_W, n_loc), s_ctx], axis=-1)
        p = jax.nn.softmax(s, axis=-1).astype(v_win.dtype)
        p_loc = p[..., :n_loc].reshape(b, NA_HEADS, GRID_W, wh, NA_WIN_W)
        return (jnp.einsum('bhqrw,brqwhv->bqhv', p_loc, v_win)
                + jnp.einsum('bhqk,bkhv->bqhv', p[..., n_loc:], vc))

    o = lax.map(row_block, (jnp.arange(rows), jnp.moveaxis(grid(ql), 1, 0)))
    y_lat = jnp.moveaxis(o, 0, 1).reshape(b, n, NA_HEADS * NA_DH) @ w_out
    if not need_ctx:
        return None, y_lat
    y_ctx = softmax_attend(qc, kc, vc, scale).reshape(b, -1, NA_HEADS * NA_DH) @ w_out
    return y_ctx, y_lat


def split_qk_norm(t, g):
    return jnp.concatenate([rms_norm(t[..., :MLA_NOPE]), rms_norm(t[..., MLA_NOPE:])], axis=-1) * g.astype(t.dtype)


def rotate_rope_part(t):
    return jnp.concatenate([t[..., :MLA_NOPE], rope_2d(t[..., MLA_NOPE:])], axis=-1)


def mla_mixer(h_ctx, h_lat, w_down, q_norm_g, kv_norm_g, w_uq, w_ukv, qk_norm_q, qk_norm_k,
              w_out, need_ctx):
    def keys_values(z, rotate):
        b, n, _ = z.shape
        c_kv = z[..., MLA_Q_RANK:MLA_Q_RANK + MLA_KV_RANK]
        k_rope = z[..., MLA_Q_RANK + MLA_KV_RANK:]
        kv = (rms_norm_gain(c_kv, kv_norm_g) @ w_ukv).reshape(b, n, MLA_HEADS, MLA_NOPE + MLA_V)
        k_rope = jnp.broadcast_to(k_rope[:, :, None, :], (b, n, MLA_HEADS, MLA_ROPE))
        k = split_qk_norm(jnp.concatenate([kv[..., :MLA_NOPE], k_rope], axis=-1), qk_norm_k)
        return (rotate_rope_part(k) if rotate else k), kv[..., MLA_NOPE:]

    def queries(z, rotate):
        b, n, _ = z.shape
        q = (rms_norm_gain(z[..., :MLA_Q_RANK], q_norm_g) @ w_uq).reshape(b, n, MLA_HEADS, MLA_QK)
        q = split_qk_norm(q, qk_norm_q)
        return rotate_rope_part(q) if rotate else q

    scale = MLA_QK ** -0.5
    z_ctx = h_ctx @ w_down
    z_lat = h_lat @ w_down
    kc, vc = keys_values(z_ctx, False)
    kl, vl = keys_values(z_lat, True)
    ql = queries(z_lat, True)
    k_all = jnp.concatenate([kc, kl], axis=1)
    v_all = jnp.concatenate([vc, vl], axis=1)
    b, n = ql.shape[0], ql.shape[1]
    o = map_query_blocks(lambda qb: softmax_attend(qb, k_all, v_all, scale), ql)
    y_lat = o.reshape(b, n, MLA_HEADS * MLA_V) @ w_out
    if not need_ctx:
        return None, y_lat
    o_ctx = softmax_attend(queries(z_ctx, False), kc, vc, scale)
    return o_ctx.reshape(b, -1, MLA_HEADS * MLA_V) @ w_out, y_lat


def setup_inputs(seed: int = 0) -> dict:
    key = jax.random.key(seed)
    keys = iter(jax.random.split(key, 40))
    f32 = jnp.float32

    def normal(shape, scale):
        return jax.random.normal(next(keys), shape, f32) * scale

    def gain(shape):
        return 1.0 + normal(shape, 0.02)

    d = D_MODEL
    n_ret, n_diff, n_na, n_mla = [len(range(m, DEPTH, N_MIXERS)) for m in range(N_MIXERS)]
    gam = 1.0 - 2.0 ** (-5.0 - np.arange(RET_HEADS, dtype=np.float32))
    decay_logit0 = jnp.asarray(np.log(gam / (1.0 - gam)), f32)
    ret_cols = 2 * RET_HEADS * RET_DK + 2 * RET_HEADS * RET_DV
    diff_cols = 2 * DIFF_HEADS * 2 * DIFF_DH + DIFF_HEADS * DIFF_DV
    mla_down_cols = MLA_Q_RANK + MLA_KV_RANK + MLA_ROPE
    return {
        'x': normal((BATCH, SEQ, d), 1.0),
        'c': normal((BATCH, d), 1.0),
        'ctx': normal((BATCH, CTX_LEN, d), 1.0),
        'c_ctx': normal((d,), 1.0),
        'ada_w': normal((DEPTH, d, ADA_CHUNKS * d), 0.5 * d ** -0.5),
        'ada_b': normal((DEPTH, ADA_CHUNKS * d), 0.02),
        'ret_w_in': normal((n_ret, d, ret_cols), d ** -0.5),
        'ret_decay_logit': decay_logit0 + normal((n_ret, 2, RET_HEADS), 0.1),
        'ret_norm_g': gain((n_ret, RET_HEADS * RET_DV)),
        'ret_w_out': normal((n_ret, RET_HEADS * RET_DV, d), (RET_HEADS * RET_DV) ** -0.5),
        'diff_w_in': normal((n_diff, d, diff_cols), d ** -0.5),
        'diff_q_norm_g': gain((n_diff, DIFF_DH)),
        'diff_k_norm_g': gain((n_diff, DIFF_DH)),
        'diff_lambda': normal((n_diff, 4, DIFF_DH), 0.1),
        'diff_subln_g': gain((n_diff, DIFF_DV)),
        'diff_w_out': normal((n_diff, DIFF_HEADS * DIFF_DV, d), (DIFF_HEADS * DIFF_DV) ** -0.5),
        'na_w_in': normal((n_na, d, 3 * NA_HEADS * NA_DH), d ** -0.5),
        'na_q_norm_g': gain((n_na, NA_DH)),
        'na_k_norm_g': gain((n_na, NA_DH)),
        'na_rel_bias': normal((n_na, NA_HEADS, 2 * NA_WIN_H - 1, 2 * NA_WIN_W - 1), 0.1),
        'na_w_out': normal((n_na, NA_HEADS * NA_DH, d), (NA_HEADS * NA_DH) ** -0.5),
        'mla_w_down': normal((n_mla, d, mla_down_cols), d ** -0.5),
        'mla_q_norm_g': gain((n_mla, MLA_Q_RANK)),
        'mla_kv_norm_g': gain((n_mla, MLA_KV_RANK)),
        'mla_w_uq': normal((n_mla, MLA_Q_RANK, MLA_HEADS * MLA_QK), MLA_Q_RANK ** -0.5),
        'mla_w_ukv': normal((n_mla, MLA_KV_RANK, MLA_HEADS * (MLA_NOPE + MLA_V)), MLA_KV_RANK ** -0.5),
        'mla_qk_norm_q': gain((n_mla, MLA_QK)),
        'mla_qk_norm_k': gain((n_mla, MLA_QK)),
        'mla_w_out': normal((n_mla, MLA_HEADS * MLA_V, d), (MLA_HEADS * MLA_V) ** -0.5),
        'ffn_w_in': normal((DEPTH, d, 2 * FFN_HIDDEN), d ** -0.5),
        'ffn_w_out': normal((DEPTH, FFN_HIDDEN, d), FFN_HIDDEN ** -0.5),
    }


def reference(x, c, ctx, c_ctx, ada_w, ada_b,
              ret_w_in, ret_decay_logit, ret_norm_g, ret_w_out,
              diff_w_in, diff_q_norm_g, diff_k_norm_g, diff_lambda, diff_subln_g, diff_w_out,
              na_w_in, na_q_norm_g, na_k_norm_g, na_rel_bias, na_w_out,
              mla_w_down, mla_q_norm_g, mla_kv_norm_g, mla_w_uq, mla_w_ukv,
              mla_qk_norm_q, mla_qk_norm_k, mla_w_out,
              ffn_w_in, ffn_w_out):
    s_lat = jax.nn.silu(c)[:, None, :]
    s_ctx = jax.nn.silu(c_ctx)[None, None, :]
    x_lat, x_ctx = x, ctx
    for i in range(DEPTH):
        need_ctx = i < DEPTH - 1
        mod_l = jnp.split(s_lat @ ada_w[i] + ada_b[i], ADA_CHUNKS, axis=-1)
        mod_c = jnp.split(s_ctx @ ada_w[i] + ada_b[i], ADA_CHUNKS, axis=-1)
        h_lat = modulate(x_lat, mod_l[0], mod_l[1])
        h_ctx = modulate(x_ctx, mod_c[0], mod_c[1])
        kind, j = i % N_MIXERS, i // N_MIXERS
        if kind == 0:
            y_ctx, y_lat = retention_mixer(h_ctx, h_lat, ret_w_in[j], ret_decay_logit[j],
                                           ret_norm_g[j], ret_w_out[j], need_ctx)
        elif kind == 1:
            lambda_init = 0.8 - 0.6 * math.exp(-0.3 * i)
            y_ctx, y_lat = diff_attention_mixer(h_ctx, h_lat, diff_w_in[j], diff_q_norm_g[j],
                                                diff_k_norm_g[j], diff_lambda[j], diff_subln_g[j],
                                                diff_w_out[j], lambda_init, need_ctx)
        elif kind == 2:
            y_ctx, y_lat = neighborhood_mixer(h_ctx, h_lat, na_w_in[j], na_q_norm_g[j],
                                              na_k_norm_g[j], na_rel_bias[j], na_w_out[j], need_ctx)
        else:
            y_ctx, y_lat = mla_mixer(h_ctx, h_lat, mla_w_down[j], mla_q_norm_g[j], mla_kv_norm_g[j],
                                     mla_w_uq[j], mla_w_ukv[j], mla_qk_norm_q[j], mla_qk_norm_k[j],
                                     mla_w_out[j], need_ctx)
        x_lat = x_lat + mod_l[2] * y_lat
        x_lat = x_lat + mod_l[5] * swiglu(modulate(x_lat, mod_l[3], mod_l[4]), ffn_w_in[i], ffn_w_out[i])
        if need_ctx:
            x_ctx = x_ctx + mod_c[2] * y_ctx
            x_ctx = x_ctx + mod_c[5] * swiglu(modulate(x_ctx, mod_c[3], mod_c[4]), ffn_w_in[i], ffn_w_out[i])
    return x_lat
```

```python
import functools
import math

import numpy as np
import jax
import jax.numpy as jnp
from jax import lax
from jax.experimental import pallas as pl
from jax.experimental.pallas import tpu as pltpu

F32 = jnp.float32
BF16 = jnp.bfloat16

GRID_W = 64
ROPE_BASE = 10000.0
EPS = 1e-6
ADA_CHUNKS = 6
N_MIXERS = 4
RET_HEADS = 4
RET_CHUNK = 128
DIFF_HEADS = 8
NA_HEADS = 16
NA_WIN_H = 8
NA_WIN_W = 16
MLA_HEADS = 16
MLA_Q_RANK = 256
MLA_KV_RANK = 128
MLA_NOPE = 64
MLA_ROPE = 32
MLA_V = 64

LANES = 128
LOG2E = 1.4426950408889634
NEG = -1e30
V7X_VMEM_BYTES = 64 * 1024 * 1024
VMEM_LIMIT = V7X_VMEM_BYTES - 8 * 1024 * 1024

TM = 256
TQ = 512
TK = 512
NA_ROWS = 8

_NT = (((1,), (1,)), ((), ()))
_TN = (((0,), (0,)), ((), ()))


def _cparams(n_axes):
    return pltpu.CompilerParams(dimension_semantics=("arbitrary",) * n_axes,
                                vmem_limit_bytes=VMEM_LIMIT)


def _const_spec(a):
    nd = a.ndim
    return pl.BlockSpec(a.shape, lambda *_: (0,) * nd, pipeline_mode=pl.Buffered(1))


def _sigmoid(v):
    return 1.0 / (1.0 + jnp.exp(-v))


def _modulate(xv, shift, scale):
    ms = jnp.mean(xv * xv, axis=-1, keepdims=True)
    return xv * lax.rsqrt(ms + EPS) * (1.0 + scale) + shift


def _dot(a, b):
    return jnp.dot(a, b, preferred_element_type=F32)


def _group_meansq(z, g):
    z2 = z * z
    hi = z2.astype(BF16)
    lo = (z2 - hi.astype(F32)).astype(BF16)
    return _dot(hi, g) + _dot(lo, g)


def _ada_kernel(c_ref, w_ref, b_ref, o_ref):
    cv = c_ref[...]
    s = cv * _sigmoid(cv)
    o_ref[0] = _dot(s, w_ref[0]) + b_ref[0]


def _ada_call(cs, ada_w, ada_b):
    depth, d, n = ada_w.shape
    tn = n // 4
    return pl.pallas_call(
        _ada_kernel,
        out_shape=jax.ShapeDtypeStruct((depth, cs.shape[0], n), F32),
        grid=(depth, n // tn),
        in_specs=[pl.BlockSpec(cs.shape, lambda l, j: (0, 0)),
                  pl.BlockSpec((1, d, tn), lambda l, j: (l, 0, j)),
                  pl.BlockSpec((1, 1, tn), lambda l, j: (l, 0, j))],
        out_specs=pl.BlockSpec((1, cs.shape[0], tn), lambda l, j: (l, 0, j)),
        compiler_params=_cparams(2),
        name="ada_mod",
    )(cs, ada_w, ada_b.reshape(depth, 1, n))


def _pre_ret_kernel(x_ref, mod_ref, cs_ref, w_ref, q_ref, k_ref, v_ref, g_ref):
    mod = mod_ref[0]
    h = _modulate(x_ref[0], mod[0:1], mod[1:2]).astype(BF16)
    dk = q_ref.shape[2] // RET_HEADS
    nqk = RET_HEADS * dk
    nv = v_ref.shape[2]
    cos = cs_ref[:, 0:dk]
    sin = cs_ref[:, dk:2 * dk]
    zq = _dot(h, w_ref[:, 0:nqk])
    zk = _dot(h, w_ref[:, nqk:2 * nqk])
    o2 = 2 * nqk + 2 * nv
    zqp = _dot(h, w_ref[:, o2:o2 + nqk])
    zkp = _dot(h, w_ref[:, o2 + nqk:o2 + 2 * nqk])
    kscale = dk ** -0.5
    for hh in range(RET_HEADS):
        sl = slice(hh * dk, (hh + 1) * dk)
        q_ref[0, :, sl] = (zq[:, sl] * cos + zqp[:, sl] * sin).astype(BF16)
        k_ref[0, :, sl] = ((zk[:, sl] * cos + zkp[:, sl] * sin) * kscale).astype(BF16)
    v_ref[0] = _dot(h, w_ref[:, 2 * nqk:2 * nqk + nv]).astype(BF16)
    g = _dot(h, w_ref[:, 2 * nqk + nv:o2])
    g_ref[0] = (g * _sigmoid(g)).astype(BF16)


def _pre_diff_kernel(x_ref, mod_ref, cs_ref, w_ref, gq_ref, gk_ref, gm_ref,
                     qa_ref, qb_ref, k_ref, v_ref, *, qscale):
    mod = mod_ref[0]
    h = _modulate(x_ref[0], mod[0:1], mod[1:2]).astype(BF16)
    n = k_ref.shape[2]
    tm = x_ref.shape[1]
    cos = cs_ref[:, 0:LANES]
    sin = cs_ref[:, LANES:2 * LANES]
    gq = gq_ref[...]
    gk = gk_ref[...]
    qc, qs = gq[0:1] * cos * qscale, gq[1:2] * sin * qscale
    kc, ks = gk[0:1] * cos, gk[1:2] * sin
    gm = gm_ref[...]
    first = lax.broadcasted_iota(jnp.int32, (tm, LANES), 1) < LANES // 2
    zq = _dot(h, w_ref[:, 0:n])
    zk = _dot(h, w_ref[:, n:2 * n])
    zqp = _dot(h, w_ref[:, 3 * n:4 * n])
    zkp = _dot(h, w_ref[:, 4 * n:5 * n])
    for hh in range(n // LANES):
        sl = slice(hh * LANES, (hh + 1) * LANES)
        z = zq[:, sl]
        q = lax.rsqrt(_group_meansq(z, gm) + EPS) * (z * qc + zqp[:, sl] * qs)
        qa_ref[0, :, sl] = jnp.where(first, q, 0.0).astype(BF16)
        qb_ref[0, :, sl] = jnp.where(first, 0.0, q).astype(BF16)
        z = zk[:, sl]
        k = lax.rsqrt(_group_meansq(z, gm) + EPS) * (z * kc + zkp[:, sl] * ks)
        k_ref[0, :, sl] = k.astype(BF16)
    v_ref[0] = _dot(h, w_ref[:, 2 * n:3 * n]).astype(BF16)


def _pre_na_kernel(x_ref, mod_ref, w_ref, gq_ref, gk_ref, gm_ref,
                   q_ref, k_ref, v_ref, *, qscale):
    mod = mod_ref[0]
    h = _modulate(x_ref[0], mod[0:1], mod[1:2]).astype(BF16)
    n = k_ref.shape[2]
    gq = gq_ref[...] * qscale
    gk = gk_ref[...]
    gm = gm_ref[...]
    zq = _dot(h, w_ref[:, 0:n])
    zk = _dot(h, w_ref[:, n:2 * n])
    for hh in range(n // LANES):
        sl = slice(hh * LANES, (hh + 1) * LANES)
        z = zq[:, sl]
        q_ref[0, :, sl] = (z * lax.rsqrt(_group_meansq(z, gm) + EPS) * gq).astype(BF16)
        z = zk[:, sl]
        k_ref[0, :, sl] = (z * lax.rsqrt(_group_meansq(z, gm) + EPS) * gk).astype(BF16)
    v_ref[0] = _dot(h, w_ref[:, 2 * n:3 * n]).astype(BF16)


def _pre_mla_kernel(x_ref, mod_ref, cs_ref, wd_ref, gqr_ref, gkv_ref, wuq_ref, wukv_ref,
                    gq_ref, gk_ref, gmq_ref, gmk_ref, q_ref, k_ref, v_ref, *, qscale):
    mod = mod_ref[0]
    h = _modulate(x_ref[0], mod[0:1], mod[1:2]).astype(BF16)
    tm = x_ref.shape[1]
    n = q_ref.shape[2]
    z = _dot(h, wd_ref[...])
    o1 = MLA_Q_RANK
    o2 = o1 + MLA_KV_RANK
    zq = z[:, 0:o1]
    qn = (zq * lax.rsqrt(jnp.mean(zq * zq, axis=-1, keepdims=True) + EPS) * gqr_ref[...]).astype(BF16)
    zc = z[:, o1:o2]
    cn = (zc * lax.rsqrt(jnp.mean(zc * zc, axis=-1, keepdims=True) + EPS) * gkv_ref[...]).astype(BF16)
    zr = z[:, o2:o2 + LANES]
    zrp = z[:, o2 + LANES:o2 + 2 * LANES]
    cos = cs_ref[:, 0:LANES]
    sin = cs_ref[:, LANES:2 * LANES]
    gq = gq_ref[...]
    gk = gk_ref[...]
    qc, qs = gq[0:1] * cos * qscale, gq[1:2] * sin * qscale
    msr = jnp.sum(zr * zr, axis=-1, keepdims=True) * (1.0 / MLA_ROPE)
    krope = lax.rsqrt(msr + EPS) * (zr * (gk[0:1] * cos) + zrp * (gk[1:2] * sin))
    gmq = gmq_ref[...]
    gmk = gmk_ref[...]
    one_lane = (lax.broadcasted_iota(jnp.int32, (tm, LANES), 1) == MLA_V).astype(F32)
    uq = _dot(qn, wuq_ref[:, 0:n])
    uqp = _dot(qn, wuq_ref[:, n:2 * n])
    uk = _dot(cn, wukv_ref[:, 0:n])
    uv = _dot(cn, wukv_ref[:, n:2 * n])
    for hh in range(n // LANES):
        sl = slice(hh * LANES, (hh + 1) * LANES)
        zh = uq[:, sl]
        q = lax.rsqrt(_group_meansq(zh, gmq) + EPS) * (zh * qc + uqp[:, sl] * qs)
        q_ref[0, :, sl] = q.astype(BF16)
        zh = uk[:, sl]
        k = zh * lax.rsqrt(_group_meansq(zh, gmk) + EPS) * gk[0:1] + krope
        k_ref[0, :, sl] = k.astype(BF16)
        v_ref[0, :, sl] = (uv[:, sl] + one_lane).astype(BF16)


def _pre_call(kern, x, mod, tables, consts, out_widths, name):
    nb, n, d = x.shape
    tm = min(TM, n)
    assert n % tm == 0
    tok = lambda w: pl.BlockSpec((1, tm, w), lambda b, i: (b, i, 0))
    in_specs = [tok(d), pl.BlockSpec((1, ADA_CHUNKS, d), lambda b, i: (b, 0, 0))]
    in_specs += [pl.BlockSpec((tm, t.shape[1]), lambda b, i: (i, 0)) for t in tables]
    in_specs += [_const_spec(a) for a in consts]
    return pl.pallas_call(
        kern,
        out_shape=[jax.ShapeDtypeStruct((nb, n, w), BF16) for w in out_widths],
        grid=(nb, n // tm),
        in_specs=in_specs,
        out_specs=[tok(w) for w in out_widths],
        compiler_params=_cparams(2),
        name=name,
    )(x, mod, *tables, *consts)


def _scalar_vec(s):
    return jnp.full((1, 1), s, F32)


def _ret_lat_kernel(lg_ref, qf_ref, kf_ref, vf_ref, qb_ref, kb_ref, vb_ref, kc_ref, vc_ref,
                    of_ref, ob_ref, sf_s, sb_s):
    hh = pl.program_id(1)
    c = pl.program_id(2)
    lf = _scalar_vec(lg_ref[0, hh])
    lb = _scalar_vec(lg_ref[1, hh])
    cn = qf_ref.shape[1]
    n_ctx = kc_ref.shape[1]

    @pl.when(c == 0)
    def _():
        pos = lax.broadcasted_iota(jnp.int32, (n_ctx, 1), 0).astype(F32)
        kc = kc_ref[0].astype(F32)
        vc = vc_ref[0]
        wf = jnp.exp((n_ctx - 1.0 - pos) * lf)
        wb = jnp.exp(pos * lb)
        sf_s[...] = lax.dot_general((kc * wf).astype(BF16), vc, _TN, preferred_element_type=F32)
        sb_s[...] = lax.dot_general((kc * wb).astype(BF16), vc, _TN, preferred_element_type=F32)

    ii = lax.broadcasted_iota(jnp.int32, (cn, cn), 0)
    jj = lax.broadcasted_iota(jnp.int32, (cn, cn), 1)
    rel = (ii - jj).astype(F32)
    idx = lax.broadcasted_iota(jnp.int32, (cn, 1), 0).astype(F32)

    q, k, v = qf_ref[0], kf_ref[0], vf_ref[0]
    dec = jnp.where(rel >= 0, jnp.exp(jnp.maximum(rel, 0.0) * lf), 0.0)
    att = lax.dot_general(q, k, _NT, preferred_element_type=F32) * dec
    s = sf_s[...]
    of_ref[0] = _dot(att.astype(BF16), v) + _dot(q, s.astype(BF16)) * jnp.exp((idx + 1.0) * lf)
    kd = (k.astype(F32) * jnp.exp((cn - 1.0 - idx) * lf)).astype(BF16)
    sf_s[...] = s * jnp.exp(cn * lf) + lax.dot_general(kd, v, _TN, preferred_element_type=F32)

    q, k, v = qb_ref[0], kb_ref[0], vb_ref[0]
    dec = jnp.where(rel <= 0, jnp.exp(jnp.maximum(-rel, 0.0) * lb), 0.0)
    att = lax.dot_general(q, k, _NT, preferred_element_type=F32) * dec
    s = sb_s[...]
    ob_ref[0] = _dot(att.astype(BF16), v) + _dot(q, s.astype(BF16)) * jnp.exp((cn - idx) * lb)
    kd = (k.astype(F32) * jnp.exp(idx * lb)).astype(BF16)
    sb_s[...] = s * jnp.exp(cn * lb) + lax.dot_general(kd, v, _TN, preferred_element_type=F32)


def _ret_lat_call(lg, q, k, v, kc, vc):
    b, n, nqk = q.shape
    dk = nqk // RET_HEADS
    dv = v.shape[2] // RET_HEADS
    n_ctx = kc.shape[1]
    cn = RET_CHUNK
    nc = n // cn
    fwd = lambda w: pl.BlockSpec((1, cn, w), lambda bb, hh, c: (bb, c, hh))
    bwd = lambda w: pl.BlockSpec((1, cn, w), lambda bb, hh, c: (bb, nc - 1 - c, hh))
    ctx = lambda w: pl.BlockSpec((1, n_ctx, w), lambda bb, hh, c: (bb, 0, hh))
    return pl.pallas_call(
        _ret_lat_kernel,
        out_shape=[jax.ShapeDtypeStruct((b, n, RET_HEADS * dv), F32)] * 2,
        grid=(b, RET_HEADS, nc),
        in_specs=[pl.BlockSpec(memory_space=pltpu.SMEM),
                  fwd(dk), fwd(dk), fwd(dv), bwd(dk), bwd(dk), bwd(dv), ctx(dk), ctx(dv)],
        out_specs=[fwd(dv), bwd(dv)],
        scratch_shapes=[pltpu.VMEM((dk, dv), F32), pltpu.VMEM((dk, dv), F32)],
        compiler_params=_cparams(3),
        name="ret_lat",
    )(lg, q, k, v, q, k, v, kc, vc)


def _ret_ctx_kernel(lg_ref, q_ref, k_ref, v_ref, o_ref):
    hh = pl.program_id(1)
    lf = _scalar_vec(lg_ref[0, hh])
    lb = _scalar_vec(lg_ref[1, hh])
    n = q_ref.shape[1]
    ii = lax.broadcasted_iota(jnp.int32, (n, n), 0)
    jj = lax.broadcasted_iota(jnp.int32, (n, n), 1)
    rel = (ii - jj).astype(F32)
    dec = (jnp.where(rel >= 0, jnp.exp(jnp.maximum(rel, 0.0) * lf), 0.0)
           + jnp.where(rel <= 0, jnp.exp(jnp.maximum(-rel, 0.0) * lb), 0.0))
    att = lax.dot_general(q_ref[0], k_ref[0], _NT, preferred_element_type=F32) * dec
    o_ref[0] = _dot(att.astype(BF16), v_ref[0])


def _ret_ctx_call(lg, q, k, v):
    b, n, nqk = q.shape
    dk = nqk // RET_HEADS
    dv = v.shape[2] // RET_HEADS
    spec = lambda w: pl.BlockSpec((1, n, w), lambda bb, hh: (bb, 0, hh))
    return pl.pallas_call(
        _ret_ctx_kernel,
        out_shape=jax.ShapeDtypeStruct((b, n, RET_HEADS * dv), F32),
        grid=(b, RET_HEADS),
        in_specs=[pl.BlockSpec(memory_space=pltpu.SMEM), spec(dk), spec(dk), spec(dv)],
        out_specs=spec(dv),
        compiler_params=_cparams(2),
        name="ret_ctx",
    )(lg, q, k, v)


def _online_step(q, k_c, v_c, m_s, acc_s, l_s=None):
    s = lax.dot_general(q, k_c, _NT, preferred_element_type=F32)
    m_old = m_s[...]
    m_new = jnp.maximum(m_old, jnp.max(s, axis=-1, keepdims=True))
    alpha = jnp.exp2(m_old - m_new)
    p = jnp.exp2(s - m_new)
    if l_s is not None:
        l_s[...] = alpha * l_s[...] + jnp.sum(p, axis=-1, keepdims=True)
    acc_s[...] = alpha * acc_s[...] + _dot(p.astype(BF16), v_c)
    m_s[...] = m_new


def _mla_flash_kernel(*refs, n_main):
    if n_main:
        q_ref, kl_ref, vl_ref, kc_ref, vc_ref, o_ref, m_s, acc_s = refs
    else:
        q_ref, kc_ref, vc_ref, o_ref, m_s, acc_s = refs
    q = q_ref[0]
    m_s[...] = jnp.full(m_s.shape, NEG, F32)
    acc_s[...] = jnp.zeros(acc_s.shape, F32)
    if n_main:
        def body(j, carry):
            off = pl.multiple_of(j * TK, TK)
            _online_step(q, kl_ref[0, pl.ds(off, TK), :], vl_ref[0, pl.ds(off, TK), :], m_s, acc_s)
            return carry
        lax.fori_loop(0, n_main, body, 0)
    _online_step(q, kc_ref[0], vc_ref[0], m_s, acc_s)
    acc = acc_s[...]
    o_ref[0] = (acc / acc[:, MLA_V:MLA_V + 1]).astype(BF16)


def _mla_flash_call(q, k_ctx, v_ctx, k_lat=None, v_lat=None):
    b, nq, w = q.shape
    nh = w // LANES
    n_ctx = k_ctx.shape[1]
    tq = min(TQ, nq)
    assert nq % tq == 0
    qspec = pl.BlockSpec((1, tq, LANES), lambda bb, hh, i: (bb, i, hh))
    full = lambda n: pl.BlockSpec((1, n, LANES), lambda bb, hh, i: (bb, 0, hh))
    args, in_specs, n_main = [q], [qspec], 0
    if k_lat is not None:
        n_lat = k_lat.shape[1]
        assert n_lat % TK == 0
        n_main = n_lat // TK
        args += [k_lat, v_lat]
        in_specs += [full(n_lat), full(n_lat)]
    args += [k_ctx, v_ctx]
    in_specs += [full(n_ctx), full(n_ctx)]
    return pl.pallas_call(
        functools.partial(_mla_flash_kernel, n_main=n_main),
        out_shape=jax.ShapeDtypeStruct((b, nq, w), BF16),
        grid=(b, nh, nq // tq),
        in_specs=in_specs,
        out_specs=qspec,
        scratch_shapes=[pltpu.VMEM((tq, 1), F32), pltpu.VMEM((tq, LANES), F32)],
        compiler_params=_cparams(3),
        name="mla_attn" if n_main else "mla_attn_ctx",
    )(*args)


def _diff_flash_kernel(*refs, n_main, out_scale):
    if n_main:
        (lam_ref, qa_ref, qb_ref, kl_ref, vl_ref, kc_ref, vc_ref, g_ref, o_ref,
         m1, l1, a1, m2, l2, a2) = refs
    else:
        (lam_ref, qa_ref, qb_ref, kc_ref, vc_ref, g_ref, o_ref, m1, l1, a1, m2, l2, a2) = refs
    qa = qa_ref[0]
    qb = qb_ref[0]
    for m_s, l_s, a_s in ((m1, l1, a1), (m2, l2, a2)):
        m_s[...] = jnp.full(m_s.shape, NEG, F32)
        l_s[...] = jnp.zeros(l_s.shape, F32)
        a_s[...] = jnp.zeros(a_s.shape, F32)

    def both(k_c, v_c):
        _online_step(qa, k_c, v_c, m1, a1, l1)
        _online_step(qb, k_c, v_c, m2, a2, l2)

    if n_main:
        def body(j, carry):
            off = pl.multiple_of(j * TK, TK)
            both(kl_ref[0, pl.ds(off, TK), :], vl_ref[0, pl.ds(off, TK), :])
            return carry
        lax.fori_loop(0, n_main, body, 0)
    both(kc_ref[0], vc_ref[0])
    o = a1[...] / l1[...] - lam_ref[0] * (a2[...] / l2[...])
    ms = jnp.mean(o * o, axis=-1, keepdims=True)
    o_ref[0] = (o * lax.rsqrt(ms + EPS) * (g_ref[...] * out_scale)).astype(BF16)


def _diff_flash_call(lam, subln_g, out_scale, qa, qb, k_ctx, v_ctx, k_lat=None, v_lat=None):
    b, nq, w = qa.shape
    nh = w // LANES
    n_ctx = k_ctx.shape[1]
    tq = min(TQ, nq)
    assert nq % tq == 0
    qspec = pl.BlockSpec((1, tq, LANES), lambda bb, hh, i: (bb, i, hh))
    full = lambda n: pl.BlockSpec((1, n, LANES), lambda bb, hh, i: (bb, 0, hh))
    args = [lam, qa, qb]
    in_specs = [pl.BlockSpec(memory_space=pltpu.SMEM), qspec, qspec]
    n_main = 0
    if k_lat is not None:
        n_lat = k_lat.shape[1]
        assert n_lat % TK == 0
        n_main = n_lat // TK
        args += [k_lat, v_lat]
        in_specs += [full(n_lat), full(n_lat)]
    args += [k_ctx, v_ctx, subln_g]
    in_specs += [full(n_ctx), full(n_ctx), pl.BlockSpec((1, LANES), lambda bb, hh, i: (0, 0))]
    stat = [pltpu.VMEM((tq, 1), F32), pltpu.VMEM((tq, 1), F32), pltpu.VMEM((tq, LANES), F32)]
    return pl.pallas_call(
        functools.partial(_diff_flash_kernel, n_main=n_main, out_scale=out_scale),
        out_shape=jax.ShapeDtypeStruct((b, nq, w), BF16),
        grid=(b, nh, nq // tq),
        in_specs=in_specs,
        out_specs=qspec,
        scratch_shapes=stat + stat,
        compiler_params=_cparams(3),
        name="diff_attn" if n_main else "diff_attn_ctx",
    )(*args)


def _na_kernel(*refs, n_loc):
    q_ref = refs[0]
    k_loc = refs[1:1 + n_loc]
    v_loc = refs[1 + n_loc:1 + 2 * n_loc]
    rest = refs[1 + 2 * n_loc:]
    if n_loc:
        tab_ref, kc_ref, vc_ref, o_ref = rest
    else:
        kc_ref, vc_ref, o_ref = rest
    q = q_ref[0]
    tq = q.shape[0]
    first = lax.broadcasted_iota(jnp.int32, (tq, LANES), 1) < LANES // 2
    zero = jnp.zeros_like(q)
    outs = []
    for hh in range(2):
        qh = jnp.where(first, q, zero) if hh == 0 else jnp.where(first, zero, q)
        ss = []
        for j in range(n_loc):
            tkb = k_loc[j].shape[1]
            s = lax.dot_general(qh, k_loc[j][0], _NT, preferred_element_type=F32)
            ss.append(s + tab_ref[0, hh, :, j * tkb:(j + 1) * tkb])
        ss.append(lax.dot_general(qh, kc_ref[0], _NT, preferred_element_type=F32))
        m = functools.reduce(jnp.maximum, [jnp.max(s, axis=-1, keepdims=True) for s in ss])
        ps = [jnp.exp2(s - m) for s in ss]
        l = functools.reduce(lambda a, c: a + c, [jnp.sum(p, axis=-1, keepdims=True) for p in ps])
        vs = [r[0] for r in v_loc] + [vc_ref[0]]
        o = functools.reduce(lambda a, c: a + c, [_dot(p.astype(BF16), v) for p, v in zip(ps, vs)])
        outs.append(o / l)
    o_ref[0] = jnp.where(first, outs[0], outs[1]).astype(BF16)


def _na_call(q, k_ctx, v_ctx, k_lat=None, v_lat=None, table=None):
    b, nq, w = q.shape
    ng = w // LANES
    n_ctx = k_ctx.shape[1]
    if k_lat is None:
        tq, nblk, n_loc = nq, 1, 0
    else:
        tq = NA_ROWS * GRID_W
        nblk = nq // tq
        n_loc = 3
        assert nq % tq == 0 and nblk >= 3
    qspec = pl.BlockSpec((1, tq, LANES), lambda g, bb, i: (bb, i, g))
    prev = pl.BlockSpec((1, tq, LANES), lambda g, bb, i: (bb, jnp.maximum(i - 1, 0), g))
    nxt = pl.BlockSpec((1, tq, LANES), lambda g, bb, i: (bb, jnp.minimum(i + 1, nblk - 1), g))
    cspec = pl.BlockSpec((1, n_ctx, LANES), lambda g, bb, i: (bb, 0, g))
    args, in_specs = [q], [qspec]
    if n_loc:
        args += [k_lat] * 3 + [v_lat] * 3 + [table]
        variant = lambda i: jnp.where(i == 0, 0, jnp.where(i == nblk - 1, 2, 1))
        in_specs += [prev, qspec, nxt] * 2
        in_specs += [pl.BlockSpec((1, 2, tq, 3 * tq), lambda g, bb, i: (variant(i), g, 0, 0))]
    args += [k_ctx, v_ctx]
    in_specs += [cspec, cspec]
    return pl.pallas_call(
        functools.partial(_na_kernel, n_loc=n_loc),
        out_shape=jax.ShapeDtypeStruct((b, nq, w), BF16),
        grid=(ng, b, nblk),
        in_specs=in_specs,
        out_specs=qspec,
        compiler_params=_cparams(3),
        name="na_attn" if n_loc else "na_attn_ctx",
    )(*args)


def _na_table(rel_bias, rows):
    r_blk = NA_ROWS
    nblk = rows // r_blk
    wh = min(NA_WIN_H, rows)
    col = np.arange(GRID_W)
    c0 = np.clip(col - NA_WIN_W // 2, 0, GRID_W - NA_WIN_W)
    dcol = col[None, :] - col[:, None]
    col_ok = (col[None, :] >= c0[:, None]) & (col[None, :] < c0[:, None] + NA_WIN_W)
    dc_idx = np.clip(dcol + NA_WIN_W - 1, 0, 2 * NA_WIN_W - 2)
    nh = rel_bias.shape[0]
    t1 = jnp.where(col_ok[None, None], rel_bias[:, :, dc_idx] * LOG2E, NEG)
    t1 = jnp.concatenate([t1, jnp.full((nh, 1, GRID_W, GRID_W), NEG, F32)], axis=1)
    masked = 2 * NA_WIN_H - 1
    idx = np.full((3, r_blk, 3 * r_blk), masked, np.int32)
    for v, i in enumerate((0, 1, nblk - 1)):
        for qr in range(r_blk):
            r = i * r_blk + qr
            r0 = min(max(r - NA_WIN_H // 2, 0), rows - wh)
            for slot, blk in enumerate((i - 1, i, i + 1)):
                if blk < 0 or blk >= nblk:
                    continue
                for kr_l in range(r_blk):
                    kr = blk * r_blk + kr_l
                    if r0 <= kr < r0 + wh:
                        idx[v, qr, slot * r_blk + kr_l] = kr - r + NA_WIN_H - 1
    tab = t1[:, idx]
    tab = jnp.transpose(tab, (1, 0, 2, 4, 3, 5))
    return tab.reshape(3, nh, r_blk * GRID_W, 3 * r_blk * GRID_W)


def _post_kernel(*refs, n_o, ret):
    o_refs = refs[:n_o]
    refs = refs[n_o:]
    if ret:
        gs_ref, ng_ref = refs[:2]
        refs = refs[2:]
    x_ref, mod_ref, wo_ref, w1_ref, w2_ref, out_ref = refs
    mod = mod_ref[0]
    if ret:
        o = functools.reduce(lambda a, c: a + c, [r[0] for r in o_refs])
        dv = o.shape[1] // RET_HEADS
        y = None
        for hh in range(RET_HEADS):
            sl = slice(hh * dv, (hh + 1) * dv)
            oh = o[:, sl]
            d = oh - jnp.mean(oh, axis=-1, keepdims=True)
            var = jnp.mean(d * d, axis=-1, keepdims=True)
            on = d * lax.rsqrt(var + EPS) * ng_ref[:, sl] * gs_ref[0, :, sl].astype(F32)
            part = _dot(on.astype(BF16), wo_ref[sl, :])
            y = part if y is None else y + part
    else:
        y = _dot(o_refs[0][0], wo_ref[...])
    x1 = x_ref[0] + mod[2:3] * y
    h2 = _modulate(x1, mod[3:4], mod[4:5]).astype(BF16)
    fh = w2_ref.shape[0]
    ua = _dot(h2, w1_ref[:, 0:fh])
    ug = _dot(h2, w1_ref[:, fh:2 * fh])
    act = (ua * _sigmoid(ua) * ug).astype(BF16)
    out_ref[0] = x1 + mod[5:6] * _dot(act, w2_ref[...])


def _post_call(os_, x, mod, wo, w1, w2, gs=None, ng=None, name="post"):
    nb, n, d = x.shape
    tm = min(TM, n)
    assert n % tm == 0
    tok = lambda w: pl.BlockSpec((1, tm, w), lambda b, i: (b, i, 0))
    ret = gs is not None
    args = list(os_)
    in_specs = [tok(o.shape[2]) for o in os_]
    if ret:
        args += [gs, ng]
        in_specs += [tok(gs.shape[2]), _const_spec(ng)]
    args += [x, mod, wo, w1, w2]
    in_specs += [tok(d), pl.BlockSpec((1, ADA_CHUNKS, d), lambda b, i: (b, 0, 0)),
                 _const_spec(wo), _const_spec(w1), _const_spec(w2)]
    return pl.pallas_call(
        functools.partial(_post_kernel, n_o=len(os_), ret=ret),
        out_shape=jax.ShapeDtypeStruct((nb, n, d), F32),
        grid=(nb, n // tm),
        in_specs=in_specs,
        out_specs=tok(d),
        compiler_params=_cparams(2),
        name=name,
    )(*args)


def _rope_cos_sin(n, d):
    quarter = d // 4
    pos = jnp.arange(n)
    row = (pos // GRID_W).astype(F32)
    col = (pos % GRID_W).astype(F32)
    inv_freq = jnp.power(ROPE_BASE, -jnp.arange(quarter, dtype=F32) / quarter)
    ar = row[:, None] * inv_freq[None, :]
    ac = col[:, None] * inv_freq[None, :]
    cos = jnp.concatenate([jnp.cos(ar), jnp.cos(ar), jnp.cos(ac), jnp.cos(ac)], axis=-1)
    sin = jnp.concatenate([-jnp.sin(ar), jnp.sin(ar), -jnp.sin(ac), jnp.sin(ac)], axis=-1)
    return cos, sin


def _rope_partner(d):
    half, quarter = d // 2, d // 4
    l = np.arange(d)
    return np.where((l % half) < quarter, l + quarter, l - quarter)


def _tiled_partner(d, n):
    return (np.arange(n) // d) * d + _rope_partner(d)[np.arange(n) % d]


def _group_matrix(blocks):
    g = np.zeros((LANES, LANES), np.float32)
    for a, b in blocks:
        g[a:b, a:b] = 1.0 / (b - a)
    return jnp.asarray(g, BF16)


def _identity_tables(n, w):
    return jnp.concatenate([jnp.ones((n, w), F32), jnp.zeros((n, w), F32)], axis=-1)


def kernel(x, c, ctx, c_ctx, ada_w, ada_b, ret_w_in, ret_decay_logit, ret_norm_g, ret_w_out, diff_w_in, diff_q_norm_g, diff_k_norm_g, diff_lambda, diff_subln_g, diff_w_out, na_w_in, na_q_norm_g, na_k_norm_g, na_rel_bias, na_w_out, mla_w_down, mla_q_norm_g, mla_kv_norm_g, mla_w_uq, mla_w_ukv, mla_qk_norm_q, mla_qk_norm_k, mla_w_out, ffn_w_in, ffn_w_out):
    bsz, seq, d = x.shape
    n_ctx = ctx.shape[1]
    depth = ada_w.shape[0]
    assert seq % GRID_W == 0 and bsz + 1 <= 8

    cs = jnp.zeros((8, d), F32).at[:bsz].set(c).at[bsz].set(c_ctx)
    mods = _ada_call(cs, ada_w, ada_b)

    x_lat = x
    x_ctx = ctx.reshape(1, bsz * n_ctx, d)
    as_ctx = lambda a: a.reshape(bsz, n_ctx, a.shape[-1])
    g64 = _group_matrix([(0, 64), (64, 128)])

    for i in range(depth):
        need_ctx = i < depth - 1
        mod_l = mods[i, :bsz].reshape(bsz, ADA_CHUNKS, d)
        mod_c = mods[i, bsz:bsz + 1].reshape(1, ADA_CHUNKS, d)
        kind, j = i % N_MIXERS, i // N_MIXERS
        gs_l = gs_c = ng = None

        if kind == 0:
            w = ret_w_in[j]
            dk = d // RET_HEADS
            nqk = RET_HEADS * dk
            perm = _tiled_partner(dk, nqk)
            wcat = jnp.concatenate([w, w[:, :nqk][:, perm], w[:, nqk:2 * nqk][:, perm]], axis=1).astype(BF16)
            cos, sin = _rope_cos_sin(seq, dk)
            tab_l = jnp.concatenate([cos, sin], axis=-1)
            tab_c = _identity_tables(bsz * n_ctx, dk)
            widths = [nqk, nqk, 2 * nqk, 2 * nqk]
            ql, kl, vl, gs_l = _pre_call(_pre_ret_kernel, x_lat, mod_l, [tab_l], [wcat], widths, "pre_ret")
            qc, kc, vc, gs_c = _pre_call(_pre_ret_kernel, x_ctx, mod_c, [tab_c], [wcat], widths, "pre_ret_ctx")
            qc, kc, vc = as_ctx(qc), as_ctx(kc), as_ctx(vc)
            lg = jax.nn.log_sigmoid(ret_decay_logit[j].astype(F32))
            o_l = list(_ret_lat_call(lg, ql, kl, vl, kc, vc))
            o_c = [_ret_ctx_call(lg, qc, kc, vc).reshape(1, bsz * n_ctx, -1)] if need_ctx else None
            ng = ret_norm_g[j].reshape(1, -1)
            wo = ret_w_out[j].astype(BF16)
        elif kind == 1:
            w = diff_w_in[j]
            dh = d // (2 * DIFF_HEADS)
            nq = 2 * DIFF_HEADS * dh
            perm = _tiled_partner(dh, nq)
            wcat = jnp.concatenate([w, w[:, :nq][:, perm], w[:, nq:2 * nq][:, perm]], axis=1).astype(BF16)
            cos, sin = _rope_cos_sin(seq, dh)
            tab_l = jnp.concatenate([cos, cos, sin, sin], axis=-1)
            tab_c = _identity_tables(bsz * n_ctx, LANES)
            pl_ = _rope_partner(dh)
            gq = jnp.stack([jnp.tile(diff_q_norm_g[j], 2), jnp.tile(diff_q_norm_g[j][pl_], 2)])
            gk = jnp.stack([jnp.tile(diff_k_norm_g[j], 2), jnp.tile(diff_k_norm_g[j][pl_], 2)])
            kern = functools.partial(_pre_diff_kernel, qscale=dh ** -0.5 * LOG2E)
            widths = [nq, nq, nq, nq]
            qa, qb, kl, vl = _pre_call(kern, x_lat, mod_l, [tab_l], [wcat, gq, gk, g64], widths, "pre_diff")
            qac, qbc, kc, vc = _pre_call(kern, x_ctx, mod_c, [tab_c], [wcat, gq, gk, g64], widths, "pre_diff_ctx")
            qac, qbc, kc, vc = as_ctx(qac), as_ctx(qbc), as_ctx(kc), as_ctx(vc)
            lv = diff_lambda[j].astype(F32)
            lambda_init = 0.8 - 0.6 * math.exp(-0.3 * i)
            lam = (jnp.exp(jnp.sum(lv[0] * lv[1])) - jnp.exp(jnp.sum(lv[2] * lv[3])) + lambda_init).reshape(1)
            sg = diff_subln_g[j].reshape(1, -1)
            o_l = [_diff_flash_call(lam, sg, 1.0 - lambda_init, qa, qb, kc, vc, kl, vl)]
            o_c = ([_diff_flash_call(lam, sg, 1.0 - lambda_init, qac, qbc, kc, vc).reshape(1, bsz * n_ctx, -1)]
                   if need_ctx else None)
            wo = diff_w_out[j].astype(BF16)
        elif kind == 2:
            w = na_w_in[j].astype(BF16)
            dh = d // NA_HEADS
            gq = jnp.tile(na_q_norm_g[j], LANES // dh).reshape(1, LANES)
            gk = jnp.tile(na_k_norm_g[j], LANES // dh).reshape(1, LANES)
            kern = functools.partial(_pre_na_kernel, qscale=dh ** -0.5 * LOG2E)
            widths = [d, d, d]
            ql, kl, vl = _pre_call(kern, x_lat, mod_l, [], [w, gq, gk, g64], widths, "pre_na")
            qc, kc, vc = _pre_call(kern, x_ctx, mod_c, [], [w, gq, gk, g64], widths, "pre_na_ctx")
            qc, kc, vc = as_ctx(qc), as_ctx(kc), as_ctx(vc)
            table = _na_table(na_rel_bias[j].astype(F32), seq // GRID_W)
            o_l = [_na_call(ql, kc, vc, kl, vl, table)]
            o_c = [_na_call(qc, kc, vc).reshape(1, bsz * n_ctx, -1)] if need_ctx else None
            wo = na_w_out[j].astype(BF16)
        else:
            nh = MLA_HEADS
            qk = MLA_NOPE + MLA_ROPE
            wd = mla_w_down[j]
            o2 = MLA_Q_RANK + MLA_KV_RANK
            rope_cols = wd[:, o2:o2 + MLA_ROPE]
            pad = lambda a, lo, hi: jnp.pad(a, ((0, 0), (lo, hi)))
            rope_grp = pad(rope_cols, MLA_NOPE, LANES - qk)
            rope_grp_p = pad(rope_cols[:, _rope_partner(MLA_ROPE)], MLA_NOPE, LANES - qk)
            wdcat = jnp.concatenate([wd[:, :o2], rope_grp, rope_grp_p], axis=1).astype(BF16)

            def head_pad(a, width):
                r = a.shape[0]
                return jnp.pad(a.reshape(r, nh, width), ((0, 0), (0, 0), (0, LANES - width))).reshape(r, nh * LANES)

            wuq = mla_w_uq[j]
            perm = (np.arange(nh * qk) // qk) * qk + np.concatenate(
                [np.arange(MLA_NOPE), MLA_NOPE + _rope_partner(MLA_ROPE)])[np.arange(nh * qk) % qk]
            wuqcat = jnp.concatenate([head_pad(wuq, qk), head_pad(wuq[:, perm], qk)], axis=1).astype(BF16)
            wukv = mla_w_ukv[j].reshape(MLA_KV_RANK, nh, MLA_NOPE + MLA_V)
            wk = head_pad(wukv[:, :, :MLA_NOPE].reshape(MLA_KV_RANK, -1), MLA_NOPE)
            wv = head_pad(wukv[:, :, MLA_NOPE:].reshape(MLA_KV_RANK, -1), MLA_V)
            wukvcat = jnp.concatenate([wk, wv], axis=1).astype(BF16)

            def gain_rows(g):
                gp = jnp.concatenate([g[:MLA_NOPE], g[MLA_NOPE:][_rope_partner(MLA_ROPE)]])
                return jnp.stack([jnp.pad(g, (0, LANES - qk)), jnp.pad(gp, (0, LANES - qk))])

            cos, sin = _rope_cos_sin(seq, MLA_ROPE)
            ones_l = lambda n, w_: jnp.ones((n, w_), F32)
            zeros_l = lambda n, w_: jnp.zeros((n, w_), F32)
            tab_l = jnp.concatenate([ones_l(seq, MLA_NOPE), cos, ones_l(seq, LANES - qk),
                                     zeros_l(seq, MLA_NOPE), sin, zeros_l(seq, LANES - qk)], axis=-1)
            tab_c = _identity_tables(bsz * n_ctx, LANES)
            gmq = _group_matrix([(0, MLA_NOPE), (MLA_NOPE, qk)])
            gmk = _group_matrix([(0, MLA_NOPE)])
            consts = [wdcat, mla_q_norm_g[j].reshape(1, -1), mla_kv_norm_g[j].reshape(1, -1), wuqcat, wukvcat,
                      gain_rows(mla_qk_norm_q[j]), gain_rows(mla_qk_norm_k[j]), gmq, gmk]
            kern = functools.partial(_pre_mla_kernel, qscale=qk ** -0.5 * LOG2E)
            widths = [nh * LANES] * 3
            ql, kl, vl = _pre_call(kern, x_lat, mod_l, [tab_l], consts, widths, "pre_mla")
            qc, kc, vc = _pre_call(kern, x_ctx, mod_c, [tab_c], consts, widths, "pre_mla_ctx")
            qc, kc, vc = as_ctx(qc), as_ctx(kc), as_ctx(vc)
            o_l = [_mla_flash_call(ql, kc, vc, kl, vl)]
            o_c = [_mla_flash_call(qc, kc, vc).reshape(1, bsz * n_ctx, -1)] if need_ctx else None
            wo = jnp.pad(mla_w_out[j].reshape(nh, MLA_V, d), ((0, 0), (0, LANES - MLA_V), (0, 0)))
            wo = wo.reshape(nh * LANES, d).astype(BF16)

        w1 = ffn_w_in[i].astype(BF16)
        w2 = ffn_w_out[i].astype(BF16)
        x_lat = _post_call(o_l, x_lat, mod_l, wo, w1, w2, gs_l, ng, name="post")
        if need_ctx:
            x_ctx = _post_call(o_c, x_ctx, mod_c, wo, w1, w2, gs_c, ng, name="post_ctx")
    return x_lat
```

```python
import functools
import math

import numpy as np
import jax
import jax.numpy as jnp
from jax import lax
from jax.experimental import pallas as pl
from jax.experimental.pallas import tpu as pltpu

F32 = jnp.float32
BF16 = jnp.bfloat16

GRID_W = 64
ROPE_BASE = 10000.0
EPS = 1e-6
ADA_CHUNKS = 6
N_MIXERS = 4
RET_HEADS = 4
RET_CHUNK = 128
DIFF_HEADS = 8
NA_HEADS = 16
NA_WIN_H = 8
NA_WIN_W = 16
MLA_HEADS = 16
MLA_Q_RANK = 256
MLA_KV_RANK = 128
MLA_NOPE = 64
MLA_ROPE = 32
MLA_V = 64

LANES = 128
LOG2E = 1.4426950408889634
NEG = -1e30
V7X_VMEM_BYTES = 64 * 1024 * 1024
VMEM_LIMIT = V7X_VMEM_BYTES - 8 * 1024 * 1024

TM = 256
TQ = 512
TK = 512
NA_ROWS = 8

_NT = (((1,), (1,)), ((), ()))
_TN = (((0,), (0,)), ((), ()))


def _cparams(n_axes):
    return pltpu.CompilerParams(dimension_semantics=("arbitrary",) * n_axes,
                                vmem_limit_bytes=VMEM_LIMIT)


def _const_spec(a):
    nd = a.ndim
    return pl.BlockSpec(a.shape, lambda *_: (0,) * nd, pipeline_mode=pl.Buffered(1))


def _sigmoid(v):
    return 1.0 / (1.0 + jnp.exp(-v))


def _modulate(xv, shift, scale):
    ms = jnp.mean(xv * xv, axis=-1, keepdims=True)
    return xv * lax.rsqrt(ms + EPS) * (1.0 + scale) + shift


def _dot(a, b):
    return jnp.dot(a, b, preferred_element_type=F32)


def _group_meansq(z, g):
    z2 = z * z
    hi = z2.astype(BF16)
    lo = (z2 - hi.astype(F32)).astype(BF16)
    return _dot(hi, g) + _dot(lo, g)


def _ada_kernel(c_ref, w_ref, b_ref, o_ref):
    cv = c_ref[...]
    s = cv * _sigmoid(cv)
    o_ref[0] = _dot(s, w_ref[0]) + b_ref[0]


def _ada_call(cs, ada_w, ada_b):
    depth, d, n = ada_w.shape
    tn = n // 4
    return pl.pallas_call(
        _ada_kernel,
        out_shape=jax.ShapeDtypeStruct((depth, cs.shape[0], n), F32),
        grid=(depth, n // tn),
        in_specs=[pl.BlockSpec(cs.shape, lambda l, j: (0, 0)),
                  pl.BlockSpec((1, d, tn), lambda l, j: (l, 0, j)),
                  pl.BlockSpec((1, 1, tn), lambda l, j: (l, 0, j))],
        out_specs=pl.BlockSpec((1, cs.shape[0], tn), lambda l, j: (l, 0, j)),
        compiler_params=_cparams(2),
        name="ada_mod",
    )(cs, ada_w, ada_b.reshape(depth, 1, n))


def _pre_ret_kernel(x_ref, mod_ref, cs_ref, w_ref, q_ref, k_ref, v_ref, g_ref):
    mod = mod_ref[0]
    h = _modulate(x_ref[0], mod[0:1], mod[1:2]).astype(BF16)
    dk = q_ref.shape[2] // RET_HEADS
    nqk = RET_HEADS * dk
    nv = v_ref.shape[2]
    cos = cs_ref[:, 0:dk]
    sin = cs_ref[:, dk:2 * dk]
    zq = _dot(h, w_ref[:, 0:nqk])
    zk = _dot(h, w_ref[:, nqk:2 * nqk])
    o2 = 2 * nqk + 2 * nv
    zqp = _dot(h, w_ref[:, o2:o2 + nqk])
    zkp = _dot(h, w_ref[:, o2 + nqk:o2 + 2 * nqk])
    kscale = dk ** -0.5
    for hh in range(RET_HEADS):
        sl = slice(hh * dk, (hh + 1) * dk)
        q_ref[0, :, sl] = (zq[:, sl] * cos + zqp[:, sl] * sin).astype(BF16)
        k_ref[0, :, sl] = ((zk[:, sl] * cos + zkp[:, sl] * sin) * kscale).astype(BF16)
    v_ref[0] = _dot(h, w_ref[:, 2 * nqk:2 * nqk + nv]).astype(BF16)
    g = _dot(h, w_ref[:, 2 * nqk + nv:o2])
    g_ref[0] = (g * _sigmoid(g)).astype(BF16)


def _pre_diff_kernel(x_ref, mod_ref, cs_ref, w_ref, gq_ref, gk_ref, gm_ref,
                     qa_ref, qb_ref, k_ref, v_ref, *, qscale):
    mod = mod_ref[0]
    h = _modulate(x_ref[0], mod[0:1], mod[1:2]).astype(BF16)
    n = k_ref.shape[2]
    tm = x_ref.shape[1]
    cos = cs_ref[:, 0:LANES]
    sin = cs_ref[:, LANES:2 * LANES]
    gq = gq_ref[...]
    gk = gk_ref[...]
    qc, qs = gq[0:1] * cos * qscale, gq[1:2] * sin * qscale
    kc, ks = gk[0:1] * cos, gk[1:2] * sin
    gm = gm_ref[...]
    first = lax.broadcasted_iota(jnp.int32, (tm, LANES), 1) < LANES // 2
    zq = _dot(h, w_ref[:, 0:n])
    zk = _dot(h, w_ref[:, n:2 * n])
    zqp = _dot(h, w_ref[:, 3 * n:4 * n])
    zkp = _dot(h, w_ref[:, 4 * n:5 * n])
    for hh in range(n // LANES):
        sl = slice(hh * LANES, (hh + 1) * LANES)
        z = zq[:, sl]
        q = lax.rsqrt(_group_meansq(z, gm) + EPS) * (z * qc + zqp[:, sl] * qs)
        qa_ref[0, :, sl] = jnp.where(first, q, 0.0).astype(BF16)
        qb_ref[0, :, sl] = jnp.where(first, 0.0, q).astype(BF16)
        z = zk[:, sl]
        k = lax.rsqrt(_group_meansq(z, gm) + EPS) * (z * kc + zkp[:, sl] * ks)
        k_ref[0, :, sl] = k.astype(BF16)
    v_ref[0] = _dot(h, w_ref[:, 2 * n:3 * n]).astype(BF16)


def _pre_na_kernel(x_ref, mod_ref, w_ref, gq_ref, gk_ref, gm_ref,
                   q_ref, k_ref, v_ref, *, qscale):
    mod = mod_ref[0]
    h = _modulate(x_ref[0], mod[0:1], mod[1:2]).astype(BF16)
    n = k_ref.shape[2]
    gq = gq_ref[...] * qscale
    gk = gk_ref[...]
    gm = gm_ref[...]
    zq = _dot(h, w_ref[:, 0:n])
    zk = _dot(h, w_ref[:, n:2 * n])
    for hh in range(n // LANES):
        sl = slice(hh * LANES, (hh + 1) * LANES)
        z = zq[:, sl]
        q_ref[0, :, sl] = (z * lax.rsqrt(_group_meansq(z, gm) + EPS) * gq).astype(BF16)
        z = zk[:, sl]
        k_ref[0, :, sl] = (z * lax.rsqrt(_group_meansq(z, gm) + EPS) * gk).astype(BF16)
    v_ref[0] = _dot(h, w_ref[:, 2 * n:3 * n]).astype(BF16)


def _pre_mla_kernel(x_ref, mod_ref, cs_ref, wd_ref, gqr_ref, gkv_ref, wuq_ref, wukv_ref,
                    gq_ref, gk_ref, gmq_ref, gmk_ref, q_ref, k_ref, v_ref, *, qscale):
    mod = mod_ref[0]
    h = _modulate(x_ref[0], mod[0:1], mod[1:2]).astype(BF16)
    tm = x_ref.shape[1]
    n = q_ref.shape[2]
    z = _dot(h, wd_ref[...])
    o1 = MLA_Q_RANK
    o2 = o1 + MLA_KV_RANK
    zq = z[:, 0:o1]
    qn = (zq * lax.rsqrt(jnp.mean(zq * zq, axis=-1, keepdims=True) + EPS) * gqr_ref[...]).astype(BF16)
    zc = z[:, o1:o2]
    cn = (zc * lax.rsqrt(jnp.mean(zc * zc, axis=-1, keepdims=True) + EPS) * gkv_ref[...]).astype(BF16)
    zr = z[:, o2:o2 + LANES]
    zrp = z[:, o2 + LANES:o2 + 2 * LANES]
    cos = cs_ref[:, 0:LANES]
    sin = cs_ref[:, LANES:2 * LANES]
    gq = gq_ref[...]
    gk = gk_ref[...]
    qc, qs = gq[0:1] * cos * qscale, gq[1:2] * sin * qscale
    msr = jnp.sum(zr * zr, axis=-1, keepdims=True) * (1.0 / MLA_ROPE)
    krope = lax.rsqrt(msr + EPS) * (zr * (gk[0:1] * cos) + zrp * (gk[1:2] * sin))
    gmq = gmq_ref[...]
    gmk = gmk_ref[...]
    one_lane = (lax.broadcasted_iota(jnp.int32, (tm, LANES), 1) == MLA_V).astype(F32)
    uq = _dot(qn, wuq_ref[:, 0:n])
    uqp = _dot(qn, wuq_ref[:, n:2 * n])
    uk = _dot(cn, wukv_ref[:, 0:n])
    uv = _dot(cn, wukv_ref[:, n:2 * n])
    for hh in range(n // LANES):
        sl = slice(hh * LANES, (hh + 1) * LANES)
        zh = uq[:, sl]
        q = lax.rsqrt(_group_meansq(zh, gmq) + EPS) * (zh * qc + uqp[:, sl] * qs)
        q_ref[0, :, sl] = q.astype(BF16)
        zh = uk[:, sl]
        k = zh * lax.rsqrt(_group_meansq(zh, gmk) + EPS) * gk[0:1] + krope
        k_ref[0, :, sl] = k.astype(BF16)
        v_ref[0, :, sl] = (uv[:, sl] + one_lane).astype(BF16)


def _pre_call(kern, x, mod, tables, consts, out_widths, name):
    nb, n, d = x.shape
    tm = min(TM, n)
    assert n % tm == 0
    tok = lambda w: pl.BlockSpec((1, tm, w), lambda b, i: (b, i, 0))
    in_specs = [tok(d), pl.BlockSpec((1, ADA_CHUNKS, d), lambda b, i: (b, 0, 0))]
    in_specs += [pl.BlockSpec((tm, t.shape[1]), lambda b, i: (i, 0)) for t in tables]
    in_specs += [_const_spec(a) for a in consts]
    return pl.pallas_call(
        kern,
        out_shape=[jax.ShapeDtypeStruct((nb, n, w), BF16) for w in out_widths],
        grid=(nb, n // tm),
        in_specs=in_specs,
        out_specs=[tok(w) for w in out_widths],
        compiler_params=_cparams(2),
        name=name,
    )(x, mod, *tables, *consts)


def _scalar_vec(s):
    return jnp.full((1, 1), s, F32)


def _ret_lat_kernel(lg_ref, qf_ref, kf_ref, vf_ref, qb_ref, kb_ref, vb_ref, kc_ref, vc_ref,
                    of_ref, ob_ref, sf_s, sb_s):
    hh = pl.program_id(1)
    c = pl.program_id(2)
    lf = _scalar_vec(lg_ref[0, hh])
    lb = _scalar_vec(lg_ref[1, hh])
    cn = qf_ref.shape[1]
    n_ctx = kc_ref.shape[1]

    @pl.when(c == 0)
    def _():
        pos = lax.broadcasted_iota(jnp.int32, (n_ctx, 1), 0).astype(F32)
        kc = kc_ref[0].astype(F32)
        vc = vc_ref[0]
        wf = jnp.exp((n_ctx - 1.0 - pos) * lf)
        wb = jnp.exp(pos * lb)
        sf_s[...] = lax.dot_general((kc * wf).astype(BF16), vc, _TN, preferred_element_type=F32)
        sb_s[...] = lax.dot_general((kc * wb).astype(BF16), vc, _TN, preferred_element_type=F32)

    ii = lax.broadcasted_iota(jnp.int32, (cn, cn), 0)
    jj = lax.broadcasted_iota(jnp.int32, (cn, cn), 1)
    rel = (ii - jj).astype(F32)
    idx = lax.broadcasted_iota(jnp.int32, (cn, 1), 0).astype(F32)

    q, k, v = qf_ref[0], kf_ref[0], vf_ref[0]
    dec = jnp.where(rel >= 0, jnp.exp(jnp.maximum(rel, 0.0) * lf), 0.0)
    att = lax.dot_general(q, k, _NT, preferred_element_type=F32) * dec
    s = sf_s[...]
    of_ref[0] = _dot(att.astype(BF16), v) + _dot(q, s.astype(BF16)) * jnp.exp((idx + 1.0) * lf)
    kd = (k.astype(F32) * jnp.exp((cn - 1.0 - idx) * lf)).astype(BF16)
    sf_s[...] = s * jnp.exp(cn * lf) + lax.dot_general(kd, v, _TN, preferred_element_type=F32)

    q, k, v = qb_ref[0], kb_ref[0], vb_ref[0]
    dec = jnp.where(rel <= 0, jnp.exp(jnp.maximum(-rel, 0.0) * lb), 0.0)
    att = lax.dot_general(q, k, _NT, preferred_element_type=F32) * dec
    s = sb_s[...]
    ob_ref[0] = _dot(att.astype(BF16), v) + _dot(q, s.astype(BF16)) * jnp.exp((cn - idx) * lb)
    kd = (k.astype(F32) * jnp.exp(idx * lb)).astype(BF16)
    sb_s[...] = s * jnp.exp(cn * lb) + lax.dot_general(kd, v, _TN, preferred_element_type=F32)


def _ret_lat_call(lg, q, k, v, kc, vc):
    b, n, nqk = q.shape
    dk = nqk // RET_HEADS
    dv = v.shape[2] // RET_HEADS
    n_ctx = kc.shape[1]
    cn = RET_CHUNK
    nc = n // cn
    fwd = lambda w: pl.BlockSpec((1, cn, w), lambda bb, hh, c: (bb, c, hh))
    bwd = lambda w: pl.BlockSpec((1, cn, w), lambda bb, hh, c: (bb, nc - 1 - c, hh))
    ctx = lambda w: pl.BlockSpec((1, n_ctx, w), lambda bb, hh, c: (bb, 0, hh))
    return pl.pallas_call(
        _ret_lat_kernel,
        out_shape=[jax.ShapeDtypeStruct((b, n, RET_HEADS * dv), F32)] * 2,
        grid=(b, RET_HEADS, nc),
        in_specs=[pl.BlockSpec(memory_space=pltpu.SMEM),
                  fwd(dk), fwd(dk), fwd(dv), bwd(dk), bwd(dk), bwd(dv), ctx(dk), ctx(dv)],
        out_specs=[fwd(dv), bwd(dv)],
        scratch_shapes=[pltpu.VMEM((dk, dv), F32), pltpu.VMEM((dk, dv), F32)],
        compiler_params=_cparams(3),
        name="ret_lat",
    )(lg, q, k, v, q, k, v, kc, vc)


def _ret_ctx_kernel(lg_ref, q_ref, k_ref, v_ref, o_ref):
    hh = pl.program_id(1)
    lf = _scalar_vec(lg_ref[0, hh])
    lb = _scalar_vec(lg_ref[1, hh])
    n = q_ref.shape[1]
    ii = lax.broadcasted_iota(jnp.int32, (n, n), 0)
    jj = lax.broadcasted_iota(jnp.int32, (n, n), 1)
    rel = (ii - jj).astype(F32)
    dec = (jnp.where(rel >= 0, jnp.exp(jnp.maximum(rel, 0.0) * lf), 0.0)
           + jnp.where(rel <= 0, jnp.exp(jnp.maximum(-rel, 0.0) * lb), 0.0))
    att = lax.dot_general(q_ref[0], k_ref[0], _NT, preferred_element_type=F32) * dec
    o_ref[0] = _dot(att.astype(BF16), v_ref[0])


def _ret_ctx_call(lg, q, k, v):
    b, n, nqk = q.shape
    dk = nqk // RET_HEADS
    dv = v.shape[2] // RET_HEADS
    spec = lambda w: pl.BlockSpec((1, n, w), lambda bb, hh: (bb, 0, hh))
    return pl.pallas_call(
        _ret_ctx_kernel,
        out_shape=jax.ShapeDtypeStruct((b, n, RET_HEADS * dv), F32),
        grid=(b, RET_HEADS),
        in_specs=[pl.BlockSpec(memory_space=pltpu.SMEM), spec(dk), spec(dk), spec(dv)],
        out_specs=spec(dv),
        compiler_params=_cparams(2),
        name="ret_ctx",
    )(lg, q, k, v)


def _lane_tiles(v, n):
    return jnp.concatenate([v] * n, axis=1) if n > 1 else v


def _softmax_update(s, m_s, l_s):
    n = s.shape[1] // LANES
    m_prev = m_s[...]
    m_new = jnp.maximum(m_prev, jnp.max(s, axis=-1, keepdims=True))
    alpha = jnp.exp2(m_prev - m_new)
    m_s[...] = m_new
    p = jnp.exp2(s - _lane_tiles(m_new, n))
    if l_s is not None:
        part = p[:, 0:LANES]
        for t in range(1, n):
            part = part + p[:, t * LANES:(t + 1) * LANES]
        l_s[...] = alpha * l_s[...] + part
    return alpha, p


def _flash_streams(streams, kl_ref, vl_ref, kc_ref, vc_ref, n_main):
    for q, m_s, l_s, acc_s, _, _, _ in streams:
        m_s[...] = jnp.full(m_s.shape, NEG, F32)
        acc_s[...] = jnp.zeros(acc_s.shape, F32)
        if l_s is not None:
            l_s[...] = jnp.zeros(l_s.shape, F32)
        s = lax.dot_general(q, kc_ref[0], _NT, preferred_element_type=F32)
        alpha, p = _softmax_update(s, m_s, l_s)
        acc_s[...] = alpha * acc_s[...] + _dot(p.astype(BF16), vc_ref[0])
    if not n_main:
        return
    assert n_main >= 2 and n_main % 2 == 0

    def chunk(ref, j):
        off = j * TK if isinstance(j, int) else pl.multiple_of(j * TK, TK)
        return ref[0, pl.ds(off, TK), :]

    def scores(j, slot):
        k_c = chunk(kl_ref, j)
        for q, _, _, _, s_scr, _, _ in streams:
            s_scr[slot] = lax.dot_general(q, k_c, _NT, preferred_element_type=F32)

    def softmax(slot):
        for _, m_s, l_s, _, s_scr, p_scr, al_scr in streams:
            alpha, p = _softmax_update(s_scr[slot], m_s, l_s)
            al_scr[slot] = alpha
            p_scr[slot] = p.astype(BF16)

    def values(j, slot):
        v_c = chunk(vl_ref, j)
        for _, _, _, acc_s, _, p_scr, al_scr in streams:
            acc_s[...] = al_scr[slot] * acc_s[...] + _dot(p_scr[slot], v_c)

    scores(0, 0)
    scores(1, 1)
    softmax(0)

    def body(jj, carry):
        j0 = 2 * jj
        scores(j0 + 2, 0)
        softmax(1)
        values(j0, 0)
        scores(j0 + 3, 1)
        softmax(0)
        values(j0 + 1, 1)
        return carry

    lax.fori_loop(0, (n_main - 2) // 2, body, 0)
    softmax(1)
    values(n_main - 2, 0)
    values(n_main - 1, 1)


def _stream_scratch(tq, n_main, with_l):
    stat = [pltpu.VMEM((tq, LANES), F32)] * (3 if with_l else 2)
    if not n_main:
        return stat
    return stat + [pltpu.VMEM((2, tq, TK), F32), pltpu.VMEM((2, tq, TK), BF16), pltpu.VMEM((2, tq, LANES), F32)]


def _mla_flash_kernel(*refs, n_main):
    if n_main:
        q_ref, kl_ref, vl_ref, kc_ref, vc_ref, o_ref, m_s, acc_s, s_scr, p_scr, al_scr = refs
    else:
        q_ref, kc_ref, vc_ref, o_ref, m_s, acc_s = refs
        kl_ref = vl_ref = s_scr = p_scr = al_scr = None
    _flash_streams([(q_ref[0], m_s, None, acc_s, s_scr, p_scr, al_scr)], kl_ref, vl_ref, kc_ref, vc_ref, n_main)
    acc = acc_s[...]
    o_ref[0] = (acc / acc[:, MLA_V:MLA_V + 1]).astype(BF16)


def _mla_flash_call(q, k_ctx, v_ctx, k_lat=None, v_lat=None):
    b, nq, w = q.shape
    nh = w // LANES
    n_ctx = k_ctx.shape[1]
    tq = min(TQ, nq)
    assert nq % tq == 0
    qspec = pl.BlockSpec((1, tq, LANES), lambda bb, hh, i: (bb, i, hh))
    full = lambda n: pl.BlockSpec((1, n, LANES), lambda bb, hh, i: (bb, 0, hh))
    args, in_specs, n_main = [q], [qspec], 0
    if k_lat is not None:
        n_lat = k_lat.shape[1]
        assert n_lat % TK == 0
        n_main = n_lat // TK
        args += [k_lat, v_lat]
        in_specs += [full(n_lat), full(n_lat)]
    args += [k_ctx, v_ctx]
    in_specs += [full(n_ctx), full(n_ctx)]
    return pl.pallas_call(
        functools.partial(_mla_flash_kernel, n_main=n_main),
        out_shape=jax.ShapeDtypeStruct((b, nq, w), BF16),
        grid=(b, nh, nq // tq),
        in_specs=in_specs,
        out_specs=qspec,
        scratch_shapes=_stream_scratch(tq, n_main, with_l=False),
        compiler_params=_cparams(3),
        name="mla_attn" if n_main else "mla_attn_ctx",
    )(*args)


def _diff_flash_kernel(*refs, n_main, out_scale):
    if n_main:
        (lam_ref, qa_ref, qb_ref, kl_ref, vl_ref, kc_ref, vc_ref, g_ref, o_ref,
         m1, l1, a1, s1, p1, al1, m2, l2, a2, s2, p2, al2) = refs
    else:
        (lam_ref, qa_ref, qb_ref, kc_ref, vc_ref, g_ref, o_ref, m1, l1, a1, m2, l2, a2) = refs
        kl_ref = vl_ref = s1 = p1 = al1 = s2 = p2 = al2 = None
    streams = [(qa_ref[0], m1, l1, a1, s1, p1, al1), (qb_ref[0], m2, l2, a2, s2, p2, al2)]
    _flash_streams(streams, kl_ref, vl_ref, kc_ref, vc_ref, n_main)
    rowsum = lambda l_s: jnp.sum(l_s[...], axis=-1, keepdims=True)
    o = a1[...] / rowsum(l1) - lam_ref[0] * (a2[...] / rowsum(l2))
    ms = jnp.mean(o * o, axis=-1, keepdims=True)
    o_ref[0] = (o * lax.rsqrt(ms + EPS) * (g_ref[...] * out_scale)).astype(BF16)


def _diff_flash_call(lam, subln_g, out_scale, qa, qb, k_ctx, v_ctx, k_lat=None, v_lat=None):
    b, nq, w = qa.shape
    nh = w // LANES
    n_ctx = k_ctx.shape[1]
    tq = min(TQ, nq)
    assert nq % tq == 0
    qspec = pl.BlockSpec((1, tq, LANES), lambda bb, hh, i: (bb, i, hh))
    full = lambda n: pl.BlockSpec((1, n, LANES), lambda bb, hh, i: (bb, 0, hh))
    args = [lam, qa, qb]
    in_specs = [pl.BlockSpec(memory_space=pltpu.SMEM), qspec, qspec]
    n_main = 0
    if k_lat is not None:
        n_lat = k_lat.shape[1]
        assert n_lat % TK == 0
        n_main = n_lat // TK
        args += [k_lat, v_lat]
        in_specs += [full(n_lat), full(n_lat)]
    args += [k_ctx, v_ctx, subln_g]
    in_specs += [full(n_ctx), full(n_ctx), pl.BlockSpec((1, LANES), lambda bb, hh, i: (0, 0))]
    stat = _stream_scratch(tq, n_main, with_l=True)
    return pl.pallas_call(
        functools.partial(_diff_flash_kernel, n_main=n_main, out_scale=out_scale),
        out_shape=jax.ShapeDtypeStruct((b, nq, w), BF16),
        grid=(b, nh, nq // tq),
        in_specs=in_specs,
        out_specs=qspec,
        scratch_shapes=stat + stat,
        compiler_params=_cparams(3),
        name="diff_attn" if n_main else "diff_attn_ctx",
    )(*args)


def _na_kernel(*refs, n_loc):
    q_ref = refs[0]
    k_loc = refs[1:1 + n_loc]
    v_loc = refs[1 + n_loc:1 + 2 * n_loc]
    rest = refs[1 + 2 * n_loc:]
    if n_loc:
        tab_ref, kc_ref, vc_ref, o_ref = rest
    else:
        kc_ref, vc_ref, o_ref = rest
    q = q_ref[0]
    tq = q.shape[0]
    first = lax.broadcasted_iota(jnp.int32, (tq, LANES), 1) < LANES // 2
    zero = jnp.zeros_like(q)
    outs = []
    for hh in range(2):
        qh = jnp.where(first, q, zero) if hh == 0 else jnp.where(first, zero, q)
        ss = []
        for j in range(n_loc):
            tkb = k_loc[j].shape[1]
            s = lax.dot_general(qh, k_loc[j][0], _NT, preferred_element_type=F32)
            ss.append(s + tab_ref[0, hh, :, j * tkb:(j + 1) * tkb])
        ss.append(lax.dot_general(qh, kc_ref[0], _NT, preferred_element_type=F32))
        m = functools.reduce(jnp.maximum, [jnp.max(s, axis=-1, keepdims=True) for s in ss])
        ps = [jnp.exp2(s - m) for s in ss]
        l = functools.reduce(lambda a, c: a + c, [jnp.sum(p, axis=-1, keepdims=True) for p in ps])
        vs = [r[0] for r in v_loc] + [vc_ref[0]]
        o = functools.reduce(lambda a, c: a + c, [_dot(p.astype(BF16), v) for p, v in zip(ps, vs)])
        outs.append(o / l)
    o_ref[0] = jnp.where(first, outs[0], outs[1]).astype(BF16)


def _na_call(q, k_ctx, v_ctx, k_lat=None, v_lat=None, table=None):
    b, nq, w = q.shape
    ng = w // LANES
    n_ctx = k_ctx.shape[1]
    if k_lat is None:
        tq, nblk, n_loc = nq, 1, 0
    else:
        tq = NA_ROWS * GRID_W
        nblk = nq // tq
        n_loc = 3
        assert nq % tq == 0 and nblk >= 3
    qspec = pl.BlockSpec((1, tq, LANES), lambda g, bb, i: (bb, i, g))
    prev = pl.BlockSpec((1, tq, LANES), lambda g, bb, i: (bb, jnp.maximum(i - 1, 0), g))
    nxt = pl.BlockSpec((1, tq, LANES), lambda g, bb, i: (bb, jnp.minimum(i + 1, nblk - 1), g))
    cspec = pl.BlockSpec((1, n_ctx, LANES), lambda g, bb, i: (bb, 0, g))
    args, in_specs = [q], [qspec]
    if n_loc:
        args += [k_lat] * 3 + [v_lat] * 3 + [table]
        variant = lambda i: jnp.where(i == 0, 0, jnp.where(i == nblk - 1, 2, 1))
        in_specs += [prev, qspec, nxt] * 2
        in_specs += [pl.BlockSpec((1, 2, tq, 3 * tq), lambda g, bb, i: (variant(i), g, 0, 0))]
    args += [k_ctx, v_ctx]
    in_specs += [cspec, cspec]
    return pl.pallas_call(
        functools.partial(_na_kernel, n_loc=n_loc),
        out_shape=jax.ShapeDtypeStruct((b, nq, w), BF16),
        grid=(ng, b, nblk),
        in_specs=in_specs,
        out_specs=qspec,
        compiler_params=_cparams(3),
        name="na_attn" if n_loc else "na_attn_ctx",
    )(*args)


def _na_table(rel_bias, rows):
    r_blk = NA_ROWS
    nblk = rows // r_blk
    wh = min(NA_WIN_H, rows)
    col = np.arange(GRID_W)
    c0 = np.clip(col - NA_WIN_W // 2, 0, GRID_W - NA_WIN_W)
    dcol = col[None, :] - col[:, None]
    col_ok = (col[None, :] >= c0[:, None]) & (col[None, :] < c0[:, None] + NA_WIN_W)
    dc_idx = np.clip(dcol + NA_WIN_W - 1, 0, 2 * NA_WIN_W - 2)
    nh = rel_bias.shape[0]
    t1 = jnp.where(col_ok[None, None], rel_bias[:, :, dc_idx] * LOG2E, NEG)
    t1 = jnp.concatenate([t1, jnp.full((nh, 1, GRID_W, GRID_W), NEG, F32)], axis=1)
    masked = 2 * NA_WIN_H - 1
    idx = np.full((3, r_blk, 3 * r_blk), masked, np.int32)
    for v, i in enumerate((0, 1, nblk - 1)):
        for qr in range(r_blk):
            r = i * r_blk + qr
            r0 = min(max(r - NA_WIN_H // 2, 0), rows - wh)
            for slot, blk in enumerate((i - 1, i, i + 1)):
                if blk < 0 or blk >= nblk:
                    continue
                for kr_l in range(r_blk):
                    kr = blk * r_blk + kr_l
                    if r0 <= kr < r0 + wh:
                        idx[v, qr, slot * r_blk + kr_l] = kr - r + NA_WIN_H - 1
    tab = t1[:, idx]
    tab = jnp.transpose(tab, (1, 0, 2, 4, 3, 5))
    return tab.reshape(3, nh, r_blk * GRID_W, 3 * r_blk * GRID_W)


def _post_kernel(*refs, n_o, ret):
    o_refs = refs[:n_o]
    refs = refs[n_o:]
    if ret:
        gs_ref, ng_ref = refs[:2]
        refs = refs[2:]
    x_ref, mod_ref, wo_ref, w1_ref, w2_ref, out_ref = refs
    mod = mod_ref[0]
    if ret:
        o = functools.reduce(lambda a, c: a + c, [r[0] for r in o_refs])
        dv = o.shape[1] // RET_HEADS
        y = None
        for hh in range(RET_HEADS):
            sl = slice(hh * dv, (hh + 1) * dv)
            oh = o[:, sl]
            d = oh - jnp.mean(oh, axis=-1, keepdims=True)
            var = jnp.mean(d * d, axis=-1, keepdims=True)
            on = d * lax.rsqrt(var + EPS) * ng_ref[:, sl] * gs_ref[0, :, sl].astype(F32)
            part = _dot(on.astype(BF16), wo_ref[sl, :])
            y = part if y is None else y + part
    else:
        y = _dot(o_refs[0][0], wo_ref[...])
    x1 = x_ref[0] + mod[2:3] * y
    h2 = _modulate(x1, mod[3:4], mod[4:5]).astype(BF16)
    fh = w2_ref.shape[0]
    ua = _dot(h2, w1_ref[:, 0:fh])
    ug = _dot(h2, w1_ref[:, fh:2 * fh])
    act = (ua * _sigmoid(ua) * ug).astype(BF16)
    out_ref[0] = x1 + mod[5:6] * _dot(act, w2_ref[...])


def _post_call(os_, x, mod, wo, w1, w2, gs=None, ng=None, name="post"):
    nb, n, d = x.shape
    tm = min(TM, n)
    assert n % tm == 0
    tok = lambda w: pl.BlockSpec((1, tm, w), lambda b, i: (b, i, 0))
    ret = gs is not None
    args = list(os_)
    in_specs = [tok(o.shape[2]) for o in os_]
    if ret:
        args += [gs, ng]
        in_specs += [tok(gs.shape[2]), _const_spec(ng)]
    args += [x, mod, wo, w1, w2]
    in_specs += [tok(d), pl.BlockSpec((1, ADA_CHUNKS, d), lambda b, i: (b, 0, 0)),
                 _const_spec(wo), _const_spec(w1), _const_spec(w2)]
    return pl.pallas_call(
        functools.partial(_post_kernel, n_o=len(os_), ret=ret),
        out_shape=jax.ShapeDtypeStruct((nb, n, d), F32),
        grid=(nb, n // tm),
        in_specs=in_specs,
        out_specs=tok(d),
        compiler_params=_cparams(2),
        name=name,
    )(*args)


def _rope_cos_sin(n, d):
    quarter = d // 4
    pos = jnp.arange(n)
    row = (pos // GRID_W).astype(F32)
    col = (pos % GRID_W).astype(F32)
    inv_freq = jnp.power(ROPE_BASE, -jnp.arange(quarter, dtype=F32) / quarter)
    ar = row[:, None] * inv_freq[None, :]
    ac = col[:, None] * inv_freq[None, :]
    cos = jnp.concatenate([jnp.cos(ar), jnp.cos(ar), jnp.cos(ac), jnp.cos(ac)], axis=-1)
    sin = jnp.concatenate([-jnp.sin(ar), jnp.sin(ar), -jnp.sin(ac), jnp.sin(ac)], axis=-1)
    return cos, sin


def _rope_partner(d):
    half, quarter = d // 2, d // 4
    l = np.arange(d)
    return np.where((l % half) < quarter, l + quarter, l - quarter)


def _tiled_partner(d, n):
    return (np.arange(n) // d) * d + _rope_partner(d)[np.arange(n) % d]


def _group_matrix(blocks):
    g = np.zeros((LANES, LANES), np.float32)
    for a, b in blocks:
        g[a:b, a:b] = 1.0 / (b - a)
    return jnp.asarray(g, BF16)


def _identity_tables(n, w):
    return jnp.concatenate([jnp.ones((n, w), F32), jnp.zeros((n, w), F32)], axis=-1)


def kernel(x, c, ctx, c_ctx, ada_w, ada_b, ret_w_in, ret_decay_logit, ret_norm_g, ret_w_out, diff_w_in, diff_q_norm_g, diff_k_norm_g, diff_lambda, diff_subln_g, diff_w_out, na_w_in, na_q_norm_g, na_k_norm_g, na_rel_bias, na_w_out, mla_w_down, mla_q_norm_g, mla_kv_norm_g, mla_w_uq, mla_w_ukv, mla_qk_norm_q, mla_qk_norm_k, mla_w_out, ffn_w_in, ffn_w_out):
    bsz, seq, d = x.shape
    n_ctx = ctx.shape[1]
    depth = ada_w.shape[0]
    assert seq % GRID_W == 0 and bsz + 1 <= 8

    cs = jnp.zeros((8, d), F32).at[:bsz].set(c).at[bsz].set(c_ctx)
    mods = _ada_call(cs, ada_w, ada_b)

    x_lat = x
    x_ctx = ctx.reshape(1, bsz * n_ctx, d)
    as_ctx = lambda a: a.reshape(bsz, n_ctx, a.shape[-1])
    g64 = _group_matrix([(0, 64), (64, 128)])

    for i in range(depth):
        need_ctx = i < depth - 1
        mod_l = mods[i, :bsz].reshape(bsz, ADA_CHUNKS, d)
        mod_c = mods[i, bsz:bsz + 1].reshape(1, ADA_CHUNKS, d)
        kind, j = i % N_MIXERS, i // N_MIXERS
        gs_l = gs_c = ng = None

        if kind == 0:
            w = ret_w_in[j]
            dk = d // RET_HEADS
            nqk = RET_HEADS * dk
            perm = _tiled_partner(dk, nqk)
            wcat = jnp.concatenate([w, w[:, :nqk][:, perm], w[:, nqk:2 * nqk][:, perm]], axis=1).astype(BF16)
            cos, sin = _rope_cos_sin(seq, dk)
            tab_l = jnp.concatenate([cos, sin], axis=-1)
            tab_c = _identity_tables(bsz * n_ctx, dk)
            widths = [nqk, nqk, 2 * nqk, 2 * nqk]
            ql, kl, vl, gs_l = _pre_call(_pre_ret_kernel, x_lat, mod_l, [tab_l], [wcat], widths, "pre_ret")
            qc, kc, vc, gs_c = _pre_call(_pre_ret_kernel, x_ctx, mod_c, [tab_c], [wcat], widths, "pre_ret_ctx")
            qc, kc, vc = as_ctx(qc), as_ctx(kc), as_ctx(vc)
            lg = jax.nn.log_sigmoid(ret_decay_logit[j].astype(F32))
            o_l = list(_ret_lat_call(lg, ql, kl, vl, kc, vc))
            o_c = [_ret_ctx_call(lg, qc, kc, vc).reshape(1, bsz * n_ctx, -1)] if need_ctx else None
            ng = ret_norm_g[j].reshape(1, -1)
            wo = ret_w_out[j].astype(BF16)
        elif kind == 1:
            w = diff_w_in[j]
            dh = d // (2 * DIFF_HEADS)
            nq = 2 * DIFF_HEADS * dh
            perm = _tiled_partner(dh, nq)
            wcat = jnp.concatenate([w, w[:, :nq][:, perm], w[:, nq:2 * nq][:, perm]], axis=1).astype(BF16)
            cos, sin = _rope_cos_sin(seq, dh)
            tab_l = jnp.concatenate([cos, cos, sin, sin], axis=-1)
            tab_c = _identity_tables(bsz * n_ctx, LANES)
            pl_ = _rope_partner(dh)
            gq = jnp.stack([jnp.tile(diff_q_norm_g[j], 2), jnp.tile(diff_q_norm_g[j][pl_], 2)])
            gk = jnp.stack([jnp.tile(diff_k_norm_g[j], 2), jnp.tile(diff_k_norm_g[j][pl_], 2)])
            kern = functools.partial(_pre_diff_kernel, qscale=dh ** -0.5 * LOG2E)
            widths = [nq, nq, nq, nq]
            qa, qb, kl, vl = _pre_call(kern, x_lat, mod_l, [tab_l], [wcat, gq, gk, g64], widths, "pre_diff")
            qac, qbc, kc, vc = _pre_call(kern, x_ctx, mod_c, [tab_c], [wcat, gq, gk, g64], widths, "pre_diff_ctx")
            qac, qbc, kc, vc = as_ctx(qac), as_ctx(qbc), as_ctx(kc), as_ctx(vc)
            lv = diff_lambda[j].astype(F32)
            lambda_init = 0.8 - 0.6 * math.exp(-0.3 * i)
            lam = (jnp.exp(jnp.sum(lv[0] * lv[1])) - jnp.exp(jnp.sum(lv[2] * lv[3])) + lambda_init).reshape(1)
            sg = diff_subln_g[j].reshape(1, -1)
            o_l = [_diff_flash_call(lam, sg, 1.0 - lambda_init, qa, qb, kc, vc, kl, vl)]
            o_c = ([_diff_flash_call(lam, sg, 1.0 - lambda_init, qac, qbc, kc, vc).reshape(1, bsz * n_ctx, -1)]
                   if need_ctx else None)
            wo = diff_w_out[j].astype(BF16)
        elif kind == 2:
            w = na_w_in[j].astype(BF16)
            dh = d // NA_HEADS
            gq = jnp.tile(na_q_norm_g[j], LANES // dh).reshape(1, LANES)
            gk = jnp.tile(na_k_norm_g[j], LANES // dh).reshape(1, LANES)
            kern = functools.partial(_pre_na_kernel, qscale=dh ** -0.5 * LOG2E)
            widths = [d, d, d]
            ql, kl, vl = _pre_call(kern, x_lat, mod_l, [], [w, gq, gk, g64], widths, "pre_na")
            qc, kc, vc = _pre_call(kern, x_ctx, mod_c, [], [w, gq, gk, g64], widths, "pre_na_ctx")
            qc, kc, vc = as_ctx(qc), as_ctx(kc), as_ctx(vc)
            table = _na_table(na_rel_bias[j].astype(F32), seq // GRID_W)
            o_l = [_na_call(ql, kc, vc, kl, vl, table)]
            o_c = [_na_call(qc, kc, vc).reshape(1, bsz * n_ctx, -1)] if need_ctx else None
            wo = na_w_out[j].astype(BF16)
        else:
            nh = MLA_HEADS
            qk = MLA_NOPE + MLA_ROPE
            wd = mla_w_down[j]
            o2 = MLA_Q_RANK + MLA_KV_RANK
            rope_cols = wd[:, o2:o2 + MLA_ROPE]
            pad = lambda a, lo, hi: jnp.pad(a, ((0, 0), (lo, hi)))
            rope_grp = pad(rope_cols, MLA_NOPE, LANES - qk)
            rope_grp_p = pad(rope_cols[:, _rope_partner(MLA_ROPE)], MLA_NOPE, LANES - qk)
            wdcat = jnp.concatenate([wd[:, :o2], rope_grp, rope_grp_p], axis=1).astype(BF16)

            def head_pad(a, width):
                r = a.shape[0]
                return jnp.pad(a.reshape(r, nh, width), ((0, 0), (0, 0), (0, LANES - width))).reshape(r, nh * LANES)

            wuq = mla_w_uq[j]
            perm = (np.arange(nh * qk) // qk) * qk + np.concatenate(
                [np.arange(MLA_NOPE), MLA_NOPE + _rope_partner(MLA_ROPE)])[np.arange(nh * qk) % qk]
            wuqcat = jnp.concatenate([head_pad(wuq, qk), head_pad(wuq[:, perm], qk)], axis=1).astype(BF16)
            wukv = mla_w_ukv[j].reshape(MLA_KV_RANK, nh, MLA_NOPE + MLA_V)
            wk = head_pad(wukv[:, :, :MLA_NOPE].reshape(MLA_KV_RANK, -1), MLA_NOPE)
            wv = head_pad(wukv[:, :, MLA_NOPE:].reshape(MLA_KV_RANK, -1), MLA_V)
            wukvcat = jnp.concatenate([wk, wv], axis=1).astype(BF16)

            def gain_rows(g):
                gp = jnp.concatenate([g[:MLA_NOPE], g[MLA_NOPE:][_rope_partner(MLA_ROPE)]])
                return jnp.stack([jnp.pad(g, (0, LANES - qk)), jnp.pad(gp, (0, LANES - qk))])

            cos, sin = _rope_cos_sin(seq, MLA_ROPE)
            ones_l = lambda n, w_: jnp.ones((n, w_), F32)
            zeros_l = lambda n, w_: jnp.zeros((n, w_), F32)
            tab_l = jnp.concatenate([ones_l(seq, MLA_NOPE), cos, ones_l(seq, LANES - qk),
                                     zeros_l(seq, MLA_NOPE), sin, zeros_l(seq, LANES - qk)], axis=-1)
            tab_c = _identity_tables(bsz * n_ctx, LANES)
            gmq = _group_matrix([(0, MLA_NOPE), (MLA_NOPE, qk)])
            gmk = _group_matrix([(0, MLA_NOPE)])
            consts = [wdcat, mla_q_norm_g[j].reshape(1, -1), mla_kv_norm_g[j].reshape(1, -1), wuqcat, wukvcat,
                      gain_rows(mla_qk_norm_q[j]), gain_rows(mla_qk_norm_k[j]), gmq, gmk]
            kern = functools.partial(_pre_mla_kernel, qscale=qk ** -0.5 * LOG2E)
            widths = [nh * LANES] * 3
            ql, kl, vl = _pre_call(kern, x_lat, mod_l, [tab_l], consts, widths, "pre_mla")
            qc, kc, vc = _pre_call(kern, x_ctx, mod_c, [tab_c], consts, widths, "pre_mla_ctx")
            qc, kc, vc = as_ctx(qc), as_ctx(kc), as_ctx(vc)
            o_l = [_mla_flash_call(ql, kc, vc, kl, vl)]
            o_c = [_mla_flash_call(qc, kc, vc).reshape(1, bsz * n_ctx, -1)] if need_ctx else None
            wo = jnp.pad(mla_w_out[j].reshape(nh, MLA_V, d), ((0, 0), (0, LANES - MLA_V), (0, 0)))
            wo = wo.reshape(nh * LANES, d).astype(BF16)

        w1 = ffn_w_in[i].astype(BF16)
        w2 = ffn_w_out[i].astype(BF16)
        x_lat = _post_call(o_l, x_lat, mod_l, wo, w1, w2, gs_l, ng, name="post")
        if need_ctx:
            x_ctx = _post_call(o_c, x_ctx, mod_c, wo, w1, w2, gs_c, ng, name="post_ctx")
    return x_lat
```

```python
import functools
import math

import numpy as np
import jax
import jax.numpy as jnp
from jax import lax
from jax.experimental import pallas as pl
from jax.experimental.pallas import tpu as pltpu

F32 = jnp.float32
BF16 = jnp.bfloat16

GRID_W = 64
ROPE_BASE = 10000.0
EPS = 1e-6
ADA_CHUNKS = 6
N_MIXERS = 4
RET_HEADS = 4
RET_CHUNK = 128
DIFF_HEADS = 8
NA_HEADS = 16
NA_WIN_H = 8
NA_WIN_W = 16
MLA_HEADS = 16
MLA_Q_RANK = 256
MLA_KV_RANK = 128
MLA_NOPE = 64
MLA_ROPE = 32
MLA_V = 64

LANES = 128
LOG2E = 1.4426950408889634
NEG = -1e30
V7X_VMEM_BYTES = 64 * 1024 * 1024
VMEM_LIMIT = V7X_VMEM_BYTES - 8 * 1024 * 1024

TM = 256
TQ = 512
TK = 512
NA_ROWS = 4
FLASH_UNROLL = 10

_NT = (((1,), (1,)), ((), ()))
_TN = (((0,), (0,)), ((), ()))


def _cparams(n_axes):
    return pltpu.CompilerParams(dimension_semantics=("arbitrary",) * n_axes,
                                vmem_limit_bytes=VMEM_LIMIT)


def _const_spec(a):
    nd = a.ndim
    return pl.BlockSpec(a.shape, lambda *_: (0,) * nd, pipeline_mode=pl.Buffered(1))


def _sigmoid(v):
    return 1.0 / (1.0 + jnp.exp(-v))


def _modulate(xv, shift, scale):
    ms = jnp.mean(xv * xv, axis=-1, keepdims=True)
    return xv * lax.rsqrt(ms + EPS) * (1.0 + scale) + shift


def _dot(a, b):
    return jnp.dot(a, b, preferred_element_type=F32)


def _group_meansq(z, g):
    z2 = z * z
    hi = z2.astype(BF16)
    lo = (z2 - hi.astype(F32)).astype(BF16)
    return _dot(hi, g) + _dot(lo, g)


def _ada_kernel(c_ref, w_ref, b_ref, o_ref):
    cv = c_ref[...]
    s = cv * _sigmoid(cv)
    o_ref[0] = _dot(s, w_ref[0]) + b_ref[0]


def _ada_call(cs, ada_w, ada_b):
    depth, d, n = ada_w.shape
    tn = n // 4
    return pl.pallas_call(
        _ada_kernel,
        out_shape=jax.ShapeDtypeStruct((depth, cs.shape[0], n), F32),
        grid=(depth, n // tn),
        in_specs=[pl.BlockSpec(cs.shape, lambda l, j: (0, 0)),
                  pl.BlockSpec((1, d, tn), lambda l, j: (l, 0, j)),
                  pl.BlockSpec((1, 1, tn), lambda l, j: (l, 0, j))],
        out_specs=pl.BlockSpec((1, cs.shape[0], tn), lambda l, j: (l, 0, j)),
        compiler_params=_cparams(2),
        name="ada_mod",
    )(cs, ada_w, ada_b.reshape(depth, 1, n))


def _pre_ret_kernel(x_ref, mod_ref, cs_ref, w_ref, q_ref, k_ref, v_ref, g_ref):
    mod = mod_ref[0]
    h = _modulate(x_ref[0], mod[0:1], mod[1:2]).astype(BF16)
    dk = q_ref.shape[2] // RET_HEADS
    nqk = RET_HEADS * dk
    nv = v_ref.shape[2]
    cos = cs_ref[:, 0:dk]
    sin = cs_ref[:, dk:2 * dk]
    zq = _dot(h, w_ref[:, 0:nqk])
    zk = _dot(h, w_ref[:, nqk:2 * nqk])
    o2 = 2 * nqk + 2 * nv
    zqp = _dot(h, w_ref[:, o2:o2 + nqk])
    zkp = _dot(h, w_ref[:, o2 + nqk:o2 + 2 * nqk])
    kscale = dk ** -0.5
    for hh in range(RET_HEADS):
        sl = slice(hh * dk, (hh + 1) * dk)
        q_ref[0, :, sl] = (zq[:, sl] * cos + zqp[:, sl] * sin).astype(BF16)
        k_ref[0, :, sl] = ((zk[:, sl] * cos + zkp[:, sl] * sin) * kscale).astype(BF16)
    v_ref[0] = _dot(h, w_ref[:, 2 * nqk:2 * nqk + nv]).astype(BF16)
    g = _dot(h, w_ref[:, 2 * nqk + nv:o2])
    g_ref[0] = (g * _sigmoid(g)).astype(BF16)


def _pre_diff_kernel(x_ref, mod_ref, cs_ref, w_ref, gq_ref, gk_ref, gm_ref,
                     qa_ref, qb_ref, k_ref, v_ref, *, qscale):
    mod = mod_ref[0]
    h = _modulate(x_ref[0], mod[0:1], mod[1:2]).astype(BF16)
    n = k_ref.shape[2]
    tm = x_ref.shape[1]
    cos = cs_ref[:, 0:LANES]
    sin = cs_ref[:, LANES:2 * LANES]
    gq = gq_ref[...]
    gk = gk_ref[...]
    qc, qs = gq[0:1] * cos * qscale, gq[1:2] * sin * qscale
    kc, ks = gk[0:1] * cos, gk[1:2] * sin
    gm = gm_ref[...]
    first = lax.broadcasted_iota(jnp.int32, (tm, LANES), 1) < LANES // 2
    zq = _dot(h, w_ref[:, 0:n])
    zk = _dot(h, w_ref[:, n:2 * n])
    zqp = _dot(h, w_ref[:, 3 * n:4 * n])
    zkp = _dot(h, w_ref[:, 4 * n:5 * n])
    for hh in range(n // LANES):
        sl = slice(hh * LANES, (hh + 1) * LANES)
        z = zq[:, sl]
        q = lax.rsqrt(_group_meansq(z, gm) + EPS) * (z * qc + zqp[:, sl] * qs)
        qa_ref[0, :, sl] = jnp.where(first, q, 0.0).astype(BF16)
        qb_ref[0, :, sl] = jnp.where(first, 0.0, q).astype(BF16)
        z = zk[:, sl]
        k = lax.rsqrt(_group_meansq(z, gm) + EPS) * (z * kc + zkp[:, sl] * ks)
        k_ref[0, :, sl] = k.astype(BF16)
    v_ref[0] = _dot(h, w_ref[:, 2 * n:3 * n]).astype(BF16)


def _pre_na_kernel(x_ref, mod_ref, w_ref, gq_ref, gk_ref, gm_ref,
                   q_ref, k_ref, v_ref, *, qscale):
    mod = mod_ref[0]
    h = _modulate(x_ref[0], mod[0:1], mod[1:2]).astype(BF16)
    n = k_ref.shape[2]
    gq = gq_ref[...] * qscale
    gk = gk_ref[...]
    gm = gm_ref[...]
    zq = _dot(h, w_ref[:, 0:n])
    zk = _dot(h, w_ref[:, n:2 * n])
    for hh in range(n // LANES):
        sl = slice(hh * LANES, (hh + 1) * LANES)
        z = zq[:, sl]
        q_ref[0, :, sl] = (z * lax.rsqrt(_group_meansq(z, gm) + EPS) * gq).astype(BF16)
        z = zk[:, sl]
        k_ref[0, :, sl] = (z * lax.rsqrt(_group_meansq(z, gm) + EPS) * gk).astype(BF16)
    v_ref[0] = _dot(h, w_ref[:, 2 * n:3 * n]).astype(BF16)


def _pre_mla_kernel(x_ref, mod_ref, cs_ref, wd_ref, gqr_ref, gkv_ref, wuq_ref, wukv_ref,
                    gq_ref, gk_ref, gmq_ref, gmk_ref, q_ref, k_ref, v_ref, *, qscale):
    mod = mod_ref[0]
    h = _modulate(x_ref[0], mod[0:1], mod[1:2]).astype(BF16)
    tm = x_ref.shape[1]
    n = q_ref.shape[2]
    z = _dot(h, wd_ref[...])
    o1 = MLA_Q_RANK
    o2 = o1 + MLA_KV_RANK
    zq = z[:, 0:o1]
    qn = (zq * lax.rsqrt(jnp.mean(zq * zq, axis=-1, keepdims=True) + EPS) * gqr_ref[...]).astype(BF16)
    zc = z[:, o1:o2]
    cn = (zc * lax.rsqrt(jnp.mean(zc * zc, axis=-1, keepdims=True) + EPS) * gkv_ref[...]).astype(BF16)
    zr = z[:, o2:o2 + LANES]
    zrp = z[:, o2 + LANES:o2 + 2 * LANES]
    cos = cs_ref[:, 0:LANES]
    sin = cs_ref[:, LANES:2 * LANES]
    gq = gq_ref[...]
    gk = gk_ref[...]
    qc, qs = gq[0:1] * cos * qscale, gq[1:2] * sin * qscale
    msr = jnp.sum(zr * zr, axis=-1, keepdims=True) * (1.0 / MLA_ROPE)
    krope = lax.rsqrt(msr + EPS) * (zr * (gk[0:1] * cos) + zrp * (gk[1:2] * sin))
    gmq = gmq_ref[...]
    gmk = gmk_ref[...]
    one_lane = (lax.broadcasted_iota(jnp.int32, (tm, LANES), 1) == MLA_V).astype(F32)
    uq = _dot(qn, wuq_ref[:, 0:n])
    uqp = _dot(qn, wuq_ref[:, n:2 * n])
    uk = _dot(cn, wukv_ref[:, 0:n])
    uv = _dot(cn, wukv_ref[:, n:2 * n])
    for hh in range(n // LANES):
        sl = slice(hh * LANES, (hh + 1) * LANES)
        zh = uq[:, sl]
        q = lax.rsqrt(_group_meansq(zh, gmq) + EPS) * (zh * qc + uqp[:, sl] * qs)
        q_ref[0, :, sl] = q.astype(BF16)
        zh = uk[:, sl]
        k = zh * lax.rsqrt(_group_meansq(zh, gmk) + EPS) * gk[0:1] + krope
        k_ref[0, :, sl] = k.astype(BF16)
        v_ref[0, :, sl] = (uv[:, sl] + one_lane).astype(BF16)


def _pre_call(kern, x, mod, tables, consts, out_widths, name):
    nb, n, d = x.shape
    tm = min(TM, n)
    assert n % tm == 0
    tok = lambda w: pl.BlockSpec((1, tm, w), lambda b, i: (b, i, 0))
    in_specs = [tok(d), pl.BlockSpec((1, ADA_CHUNKS, d), lambda b, i: (b, 0, 0))]
    in_specs += [pl.BlockSpec((tm, t.shape[1]), lambda b, i: (i, 0)) for t in tables]
    in_specs += [_const_spec(a) for a in consts]
    return pl.pallas_call(
        kern,
        out_shape=[jax.ShapeDtypeStruct((nb, n, w), BF16) for w in out_widths],
        grid=(nb, n // tm),
        in_specs=in_specs,
        out_specs=[tok(w) for w in out_widths],
        compiler_params=_cparams(2),
        name=name,
    )(x, mod, *tables, *consts)


def _scalar_vec(s):
    return jnp.full((1, 1), s, F32)


def _ret_lat_kernel(lg_ref, qf_ref, kf_ref, vf_ref, qb_ref, kb_ref, vb_ref, kc_ref, vc_ref,
                    of_ref, ob_ref, sf_s, sb_s):
    hh = pl.program_id(1)
    c = pl.program_id(2)
    lf = _scalar_vec(lg_ref[0, hh])
    lb = _scalar_vec(lg_ref[1, hh])
    cn = qf_ref.shape[1]
    n_ctx = kc_ref.shape[1]

    @pl.when(c == 0)
    def _():
        pos = lax.broadcasted_iota(jnp.int32, (n_ctx, 1), 0).astype(F32)
        kc = kc_ref[0].astype(F32)
        vc = vc_ref[0]
        wf = jnp.exp((n_ctx - 1.0 - pos) * lf)
        wb = jnp.exp(pos * lb)
        sf_s[...] = lax.dot_general((kc * wf).astype(BF16), vc, _TN, preferred_element_type=F32)
        sb_s[...] = lax.dot_general((kc * wb).astype(BF16), vc, _TN, preferred_element_type=F32)

    ii = lax.broadcasted_iota(jnp.int32, (cn, cn), 0)
    jj = lax.broadcasted_iota(jnp.int32, (cn, cn), 1)
    rel = (ii - jj).astype(F32)
    idx = lax.broadcasted_iota(jnp.int32, (cn, 1), 0).astype(F32)

    q, k, v = qf_ref[0], kf_ref[0], vf_ref[0]
    dec = jnp.where(rel >= 0, jnp.exp(jnp.maximum(rel, 0.0) * lf), 0.0)
    att = lax.dot_general(q, k, _NT, preferred_element_type=F32) * dec
    s = sf_s[...]
    of_ref[0] = _dot(att.astype(BF16), v) + _dot(q, s.astype(BF16)) * jnp.exp((idx + 1.0) * lf)
    kd = (k.astype(F32) * jnp.exp((cn - 1.0 - idx) * lf)).astype(BF16)
    sf_s[...] = s * jnp.exp(cn * lf) + lax.dot_general(kd, v, _TN, preferred_element_type=F32)

    q, k, v = qb_ref[0], kb_ref[0], vb_ref[0]
    dec = jnp.where(rel <= 0, jnp.exp(jnp.maximum(-rel, 0.0) * lb), 0.0)
    att = lax.dot_general(q, k, _NT, preferred_element_type=F32) * dec
    s = sb_s[...]
    ob_ref[0] = _dot(att.astype(BF16), v) + _dot(q, s.astype(BF16)) * jnp.exp((cn - idx) * lb)
    kd = (k.astype(F32) * jnp.exp(idx * lb)).astype(BF16)
    sb_s[...] = s * jnp.exp(cn * lb) + lax.dot_general(kd, v, _TN, preferred_element_type=F32)


def _ret_lat_call(lg, q, k, v, kc, vc):
    b, n, nqk = q.shape
    dk = nqk // RET_HEADS
    dv = v.shape[2] // RET_HEADS
    n_ctx = kc.shape[1]
    cn = RET_CHUNK
    nc = n // cn
    fwd = lambda w: pl.BlockSpec((1, cn, w), lambda bb, hh, c: (bb, c, hh))
    bwd = lambda w: pl.BlockSpec((1, cn, w), lambda bb, hh, c: (bb, nc - 1 - c, hh))
    ctx = lambda w: pl.BlockSpec((1, n_ctx, w), lambda bb, hh, c: (bb, 0, hh))
    return pl.pallas_call(
        _ret_lat_kernel,
        out_shape=[jax.ShapeDtypeStruct((b, n, RET_HEADS * dv), F32)] * 2,
        grid=(b, RET_HEADS, nc),
        in_specs=[pl.BlockSpec(memory_space=pltpu.SMEM),
                  fwd(dk), fwd(dk), fwd(dv), bwd(dk), bwd(dk), bwd(dv), ctx(dk), ctx(dv)],
        out_specs=[fwd(dv), bwd(dv)],
        scratch_shapes=[pltpu.VMEM((dk, dv), F32), pltpu.VMEM((dk, dv), F32)],
        compiler_params=_cparams(3),
        name="ret_lat",
    )(lg, q, k, v, q, k, v, kc, vc)


def _ret_ctx_kernel(lg_ref, q_ref, k_ref, v_ref, o_ref):
    hh = pl.program_id(1)
    lf = _scalar_vec(lg_ref[0, hh])
    lb = _scalar_vec(lg_ref[1, hh])
    n = q_ref.shape[1]
    ii = lax.broadcasted_iota(jnp.int32, (n, n), 0)
    jj = lax.broadcasted_iota(jnp.int32, (n, n), 1)
    rel = (ii - jj).astype(F32)
    dec = (jnp.where(rel >= 0, jnp.exp(jnp.maximum(rel, 0.0) * lf), 0.0)
           + jnp.where(rel <= 0, jnp.exp(jnp.maximum(-rel, 0.0) * lb), 0.0))
    att = lax.dot_general(q_ref[0], k_ref[0], _NT, preferred_element_type=F32) * dec
    o_ref[0] = _dot(att.astype(BF16), v_ref[0])


def _ret_ctx_call(lg, q, k, v):
    b, n, nqk = q.shape
    dk = nqk // RET_HEADS
    dv = v.shape[2] // RET_HEADS
    spec = lambda w: pl.BlockSpec((1, n, w), lambda bb, hh: (bb, 0, hh))
    return pl.pallas_call(
        _ret_ctx_kernel,
        out_shape=jax.ShapeDtypeStruct((b, n, RET_HEADS * dv), F32),
        grid=(b, RET_HEADS),
        in_specs=[pl.BlockSpec(memory_space=pltpu.SMEM), spec(dk), spec(dk), spec(dv)],
        out_specs=spec(dv),
        compiler_params=_cparams(2),
        name="ret_ctx",
    )(lg, q, k, v)


def _lane_tiles(v, n):
    return jnp.concatenate([v] * n, axis=1) if n > 1 else v


def _softmax_update(s, m_s, l_s):
    n = s.shape[1] // LANES
    m_prev = m_s[...]
    m_new = jnp.maximum(m_prev, jnp.max(s, axis=-1, keepdims=True))
    alpha = jnp.exp2(m_prev - m_new)
    m_s[...] = m_new
    p = jnp.exp2(s - _lane_tiles(m_new, n))
    if l_s is not None:
        part = p[:, 0:LANES]
        for t in range(1, n):
            part = part + p[:, t * LANES:(t + 1) * LANES]
        l_s[...] = alpha * l_s[...] + part
    return alpha, p


def _flash_streams(streams, kl_ref, vl_ref, kc_ref, vc_ref, n_main):
    for q, m_s, l_s, acc_s, _, _, _ in streams:
        m_s[...] = jnp.full(m_s.shape, NEG, F32)
        acc_s[...] = jnp.zeros(acc_s.shape, F32)
        if l_s is not None:
            l_s[...] = jnp.zeros(l_s.shape, F32)
        s = lax.dot_general(q, kc_ref[0], _NT, preferred_element_type=F32)
        alpha, p = _softmax_update(s, m_s, l_s)
        acc_s[...] = alpha * acc_s[...] + _dot(p.astype(BF16), vc_ref[0])
    if not n_main:
        return
    assert n_main >= 2 and n_main % 2 == 0

    def chunk(ref, j):
        off = j * TK if isinstance(j, int) else pl.multiple_of(j * TK, TK)
        return ref[0, pl.ds(off, TK), :]

    def scores(j, slot):
        k_c = chunk(kl_ref, j)
        for q, _, _, _, s_scr, _, _ in streams:
            s_scr[slot] = lax.dot_general(q, k_c, _NT, preferred_element_type=F32)

    def softmax(slot):
        for _, m_s, l_s, _, s_scr, p_scr, al_scr in streams:
            alpha, p = _softmax_update(s_scr[slot], m_s, l_s)
            al_scr[slot] = alpha
            p_scr[slot] = p.astype(BF16)

    def values(j, slot):
        v_c = chunk(vl_ref, j)
        for _, _, _, acc_s, _, p_scr, al_scr in streams:
            acc_s[...] = al_scr[slot] * acc_s[...] + _dot(p_scr[slot], v_c)

    scores(0, 0)
    scores(1, 1)
    softmax(0)

    def step(j, par):
        scores(j + 2, par)
        softmax(1 - par)
        values(j, par)

    n_pipe = n_main - 2
    n_loop = n_pipe // FLASH_UNROLL

    def body(t, carry):
        for u in range(FLASH_UNROLL):
            step(t * FLASH_UNROLL + u, u % 2)
        return carry

    if n_loop:
        lax.fori_loop(0, n_loop, body, 0)
    for j in range(n_loop * FLASH_UNROLL, n_pipe):
        step(j, j % 2)
    softmax(1)
    values(n_main - 2, 0)
    values(n_main - 1, 1)


def _stream_scratch(tq, n_main, with_l):
    stat = [pltpu.VMEM((tq, LANES), F32)] * (3 if with_l else 2)
    if not n_main:
        return stat
    return stat + [pltpu.VMEM((2, tq, TK), F32), pltpu.VMEM((2, tq, TK), BF16), pltpu.VMEM((2, tq, LANES), F32)]


def _mla_flash_kernel(*refs, n_main):
    if n_main:
        q_ref, kl_ref, vl_ref, kc_ref, vc_ref, o_ref, m_s, acc_s, s_scr, p_scr, al_scr = refs
    else:
        q_ref, kc_ref, vc_ref, o_ref, m_s, acc_s = refs
        kl_ref = vl_ref = s_scr = p_scr = al_scr = None
    _flash_streams([(q_ref[0], m_s, None, acc_s, s_scr, p_scr, al_scr)], kl_ref, vl_ref, kc_ref, vc_ref, n_main)
    acc = acc_s[...]
    o_ref[0] = (acc / acc[:, MLA_V:MLA_V + 1]).astype(BF16)


def _mla_flash_call(q, k_ctx, v_ctx, k_lat=None, v_lat=None):
    b, nq, w = q.shape
    nh = w // LANES
    n_ctx = k_ctx.shape[1]
    tq = min(TQ, nq)
    assert nq % tq == 0
    qspec = pl.BlockSpec((1, tq, LANES), lambda bb, hh, i: (bb, i, hh))
    full = lambda n: pl.BlockSpec((1, n, LANES), lambda bb, hh, i: (bb, 0, hh))
    args, in_specs, n_main = [q], [qspec], 0
    if k_lat is not None:
        n_lat = k_lat.shape[1]
        assert n_lat % TK == 0
        n_main = n_lat // TK
        args += [k_lat, v_lat]
        in_specs += [full(n_lat), full(n_lat)]
    args += [k_ctx, v_ctx]
    in_specs += [full(n_ctx), full(n_ctx)]
    return pl.pallas_call(
        functools.partial(_mla_flash_kernel, n_main=n_main),
        out_shape=jax.ShapeDtypeStruct((b, nq, w), BF16),
        grid=(b, nh, nq // tq),
        in_specs=in_specs,
        out_specs=qspec,
        scratch_shapes=_stream_scratch(tq, n_main, with_l=False),
        compiler_params=_cparams(3),
        name="mla_attn" if n_main else "mla_attn_ctx",
    )(*args)


def _diff_flash_kernel(*refs, n_main, out_scale):
    if n_main:
        (lam_ref, qa_ref, qb_ref, kl_ref, vl_ref, kc_ref, vc_ref, g_ref, o_ref,
         m1, l1, a1, m2, l2, a2, s_scr, p_scr, al_scr) = refs
    else:
        (lam_ref, qa_ref, qb_ref, kc_ref, vc_ref, g_ref, o_ref, m1, l1, a1, m2, l2, a2) = refs
        kl_ref = vl_ref = s_scr = p_scr = al_scr = None
    for q_ref, m_s, l_s, a_s in ((qa_ref, m1, l1, a1), (qb_ref, m2, l2, a2)):
        _flash_streams([(q_ref[0], m_s, l_s, a_s, s_scr, p_scr, al_scr)], kl_ref, vl_ref, kc_ref, vc_ref, n_main)
    rowsum = lambda l_s: jnp.sum(l_s[...], axis=-1, keepdims=True)
    o = a1[...] / rowsum(l1) - lam_ref[0] * (a2[...] / rowsum(l2))
    ms = jnp.mean(o * o, axis=-1, keepdims=True)
    o_ref[0] = (o * lax.rsqrt(ms + EPS) * (g_ref[...] * out_scale)).astype(BF16)


def _diff_flash_call(lam, subln_g, out_scale, qa, qb, k_ctx, v_ctx, k_lat=None, v_lat=None):
    b, nq, w = qa.shape
    nh = w // LANES
    n_ctx = k_ctx.shape[1]
    tq = min(TQ, nq)
    assert nq % tq == 0
    qspec = pl.BlockSpec((1, tq, LANES), lambda bb, hh, i: (bb, i, hh))
    full = lambda n: pl.BlockSpec((1, n, LANES), lambda bb, hh, i: (bb, 0, hh))
    args = [lam, qa, qb]
    in_specs = [pl.BlockSpec(memory_space=pltpu.SMEM), qspec, qspec]
    n_main = 0
    if k_lat is not None:
        n_lat = k_lat.shape[1]
        assert n_lat % TK == 0
        n_main = n_lat // TK
        args += [k_lat, v_lat]
        in_specs += [full(n_lat), full(n_lat)]
    args += [k_ctx, v_ctx, subln_g]
    in_specs += [full(n_ctx), full(n_ctx), pl.BlockSpec((1, LANES), lambda bb, hh, i: (0, 0))]
    stat = _stream_scratch(tq, 0, with_l=True)
    stage = _stream_scratch(tq, n_main, with_l=True)[len(stat):]
    return pl.pallas_call(
        functools.partial(_diff_flash_kernel, n_main=n_main, out_scale=out_scale),
        out_shape=jax.ShapeDtypeStruct((b, nq, w), BF16),
        grid=(b, nh, nq // tq),
        in_specs=in_specs,
        out_specs=qspec,
        scratch_shapes=stat + stat + stage,
        compiler_params=_cparams(3),
        name="diff_attn" if n_main else "diff_attn_ctx",
    )(*args)


def _na_kernel(*refs, n_loc):
    q_ref = refs[0]
    k_loc = refs[1:1 + n_loc]
    v_loc = refs[1 + n_loc:1 + 2 * n_loc]
    rest = refs[1 + 2 * n_loc:]
    if n_loc:
        tab_ref, kc_ref, vc_ref, o_ref = rest
    else:
        kc_ref, vc_ref, o_ref = rest
    q = q_ref[0]
    tq = q.shape[0]
    first = lax.broadcasted_iota(jnp.int32, (tq, LANES), 1) < LANES // 2
    zero = jnp.zeros_like(q)
    outs = []
    for hh in range(2):
        qh = jnp.where(first, q, zero) if hh == 0 else jnp.where(first, zero, q)
        ss = []
        for j in range(n_loc):
            tkb = k_loc[j].shape[1]
            s = lax.dot_general(qh, k_loc[j][0], _NT, preferred_element_type=F32)
            ss.append(s + tab_ref[0, hh, :, j * tkb:(j + 1) * tkb])
        ss.append(lax.dot_general(qh, kc_ref[0], _NT, preferred_element_type=F32))
        m = functools.reduce(jnp.maximum, [jnp.max(s, axis=-1, keepdims=True) for s in ss])
        ps = [jnp.exp2(s - m) for s in ss]
        l = functools.reduce(lambda a, c: a + c, [jnp.sum(p, axis=-1, keepdims=True) for p in ps])
        vs = [r[0] for r in v_loc] + [vc_ref[0]]
        o = functools.reduce(lambda a, c: a + c, [_dot(p.astype(BF16), v) for p, v in zip(ps, vs)])
        outs.append(o / l)
    o_ref[0] = jnp.where(first, outs[0], outs[1]).astype(BF16)


def _na_call(q, k_ctx, v_ctx, k_lat=None, v_lat=None, table=None):
    b, nq, w = q.shape
    ng = w // LANES
    n_ctx = k_ctx.shape[1]
    if k_lat is None:
        tq, nblk, n_loc = nq, 1, 0
    else:
        tq = NA_ROWS * GRID_W
        nblk = nq // tq
        n_loc = 3
        assert nq % tq == 0 and nblk >= 3
    qspec = pl.BlockSpec((1, tq, LANES), lambda g, bb, i: (bb, i, g))
    prev = pl.BlockSpec((1, tq, LANES), lambda g, bb, i: (bb, jnp.maximum(i - 1, 0), g))
    nxt = pl.BlockSpec((1, tq, LANES), lambda g, bb, i: (bb, jnp.minimum(i + 1, nblk - 1), g))
    cspec = pl.BlockSpec((1, n_ctx, LANES), lambda g, bb, i: (bb, 0, g))
    args, in_specs = [q], [qspec]
    if n_loc:
        args += [k_lat] * 3 + [v_lat] * 3 + [table]
        variant = lambda i: jnp.where(i == 0, 0, jnp.where(i == nblk - 1, 2, 1))
        in_specs += [prev, qspec, nxt] * 2
        in_specs += [pl.BlockSpec((1, 2, tq, 3 * tq), lambda g, bb, i: (variant(i), g, 0, 0))]
    args += [k_ctx, v_ctx]
    in_specs += [cspec, cspec]
    return pl.pallas_call(
        functools.partial(_na_kernel, n_loc=n_loc),
        out_shape=jax.ShapeDtypeStruct((b, nq, w), BF16),
        grid=(ng, b, nblk),
        in_specs=in_specs,
        out_specs=qspec,
        compiler_params=_cparams(3),
        name="na_attn" if n_loc else "na_attn_ctx",
    )(*args)


def _na_table(rel_bias, rows):
    r_blk = NA_ROWS
    nblk = rows // r_blk
    wh = min(NA_WIN_H, rows)
    col = np.arange(GRID_W)
    c0 = np.clip(col - NA_WIN_W // 2, 0, GRID_W - NA_WIN_W)
    dcol = col[None, :] - col[:, None]
    col_ok = (col[None, :] >= c0[:, None]) & (col[None, :] < c0[:, None] + NA_WIN_W)
    dc_idx = np.clip(dcol + NA_WIN_W - 1, 0, 2 * NA_WIN_W - 2)
    nh = rel_bias.shape[0]
    t1 = jnp.where(col_ok[None, None], rel_bias[:, :, dc_idx] * LOG2E, NEG)
    t1 = jnp.concatenate([t1, jnp.full((nh, 1, GRID_W, GRID_W), NEG, F32)], axis=1)
    masked = 2 * NA_WIN_H - 1
    idx = np.full((3, r_blk, 3 * r_blk), masked, np.int32)
    for v, i in enumerate((0, 1, nblk - 1)):
        for qr in range(r_blk):
            r = i * r_blk + qr
            r0 = min(max(r - NA_WIN_H // 2, 0), rows - wh)
            for slot, blk in enumerate((i - 1, i, i + 1)):
                if blk < 0 or blk >= nblk:
                    continue
                for kr_l in range(r_blk):
                    kr = blk * r_blk + kr_l
                    if r0 <= kr < r0 + wh:
                        idx[v, qr, slot * r_blk + kr_l] = kr - r + NA_WIN_H - 1
    tab = t1[:, idx]
    tab = jnp.transpose(tab, (1, 0, 2, 4, 3, 5))
    return tab.reshape(3, nh, r_blk * GRID_W, 3 * r_blk * GRID_W)


def _post_kernel(*refs, n_o, ret):
    o_refs = refs[:n_o]
    refs = refs[n_o:]
    if ret:
        gs_ref, ng_ref = refs[:2]
        refs = refs[2:]
    x_ref, mod_ref, wo_ref, w1_ref, w2_ref, out_ref = refs
    mod = mod_ref[0]
    if ret:
        o = functools.reduce(lambda a, c: a + c, [r[0] for r in o_refs])
        dv = o.shape[1] // RET_HEADS
        y = None
        for hh in range(RET_HEADS):
            sl = slice(hh * dv, (hh + 1) * dv)
            oh = o[:, sl]
            d = oh - jnp.mean(oh, axis=-1, keepdims=True)
            var = jnp.mean(d * d, axis=-1, keepdims=True)
            on = d * lax.rsqrt(var + EPS) * ng_ref[:, sl] * gs_ref[0, :, sl].astype(F32)
            part = _dot(on.astype(BF16), wo_ref[sl, :])
            y = part if y is None else y + part
    else:
        y = _dot(o_refs[0][0], wo_ref[...])
    x1 = x_ref[0] + mod[2:3] * y
    h2 = _modulate(x1, mod[3:4], mod[4:5]).astype(BF16)
    fh = w2_ref.shape[0]
    ua = _dot(h2, w1_ref[:, 0:fh])
    ug = _dot(h2, w1_ref[:, fh:2 * fh])
    act = (ua * _sigmoid(ua) * ug).astype(BF16)
    out_ref[0] = x1 + mod[5:6] * _dot(act, w2_ref[...])


def _post_call(os_, x, mod, wo, w1, w2, gs=None, ng=None, name="post"):
    nb, n, d = x.shape
    tm = min(TM, n)
    assert n % tm == 0
    tok = lambda w: pl.BlockSpec((1, tm, w), lambda b, i: (b, i, 0))
    ret = gs is not None
    args = list(os_)
    in_specs = [tok(o.shape[2]) for o in os_]
    if ret:
        args += [gs, ng]
        in_specs += [tok(gs.shape[2]), _const_spec(ng)]
    args += [x, mod, wo, w1, w2]
    in_specs += [tok(d), pl.BlockSpec((1, ADA_CHUNKS, d), lambda b, i: (b, 0, 0)),
                 _const_spec(wo), _const_spec(w1), _const_spec(w2)]
    return pl.pallas_call(
        functools.partial(_post_kernel, n_o=len(os_), ret=ret),
        out_shape=jax.ShapeDtypeStruct((nb, n, d), F32),
        grid=(nb, n // tm),
        in_specs=in_specs,
        out_specs=tok(d),
        compiler_params=_cparams(2),
        name=name,
    )(*args)


def _rope_cos_sin(n, d):
    quarter = d // 4
    pos = jnp.arange(n)
    row = (pos // GRID_W).astype(F32)
    col = (pos % GRID_W).astype(F32)
    inv_freq = jnp.power(ROPE_BASE, -jnp.arange(quarter, dtype=F32) / quarter)
    ar = row[:, None] * inv_freq[None, :]
    ac = col[:, None] * inv_freq[None, :]
    cos = jnp.concatenate([jnp.cos(ar), jnp.cos(ar), jnp.cos(ac), jnp.cos(ac)], axis=-1)
    sin = jnp.concatenate([-jnp.sin(ar), jnp.sin(ar), -jnp.sin(ac), jnp.sin(ac)], axis=-1)
    return cos, sin


def _rope_partner(d):
    half, quarter = d // 2, d // 4
    l = np.arange(d)
    return np.where((l % half) < quarter, l + quarter, l - quarter)


def _tiled_partner(d, n):
    return (np.arange(n) // d) * d + _rope_partner(d)[np.arange(n) % d]


def _group_matrix(blocks):
    g = np.zeros((LANES, LANES), np.float32)
    for a, b in blocks:
        g[a:b, a:b] = 1.0 / (b - a)
    return jnp.asarray(g, BF16)


def _identity_tables(n, w):
    return jnp.concatenate([jnp.ones((n, w), F32), jnp.zeros((n, w), F32)], axis=-1)


def kernel(x, c, ctx, c_ctx, ada_w, ada_b, ret_w_in, ret_decay_logit, ret_norm_g, ret_w_out, diff_w_in, diff_q_norm_g, diff_k_norm_g, diff_lambda, diff_subln_g, diff_w_out, na_w_in, na_q_norm_g, na_k_norm_g, na_rel_bias, na_w_out, mla_w_down, mla_q_norm_g, mla_kv_norm_g, mla_w_uq, mla_w_ukv, mla_qk_norm_q, mla_qk_norm_k, mla_w_out, ffn_w_in, ffn_w_out):
    bsz, seq, d = x.shape
    n_ctx = ctx.shape[1]
    depth = ada_w.shape[0]
    assert seq % GRID_W == 0 and bsz + 1 <= 8

    cs = jnp.zeros((8, d), F32).at[:bsz].set(c).at[bsz].set(c_ctx)
    mods = _ada_call(cs, ada_w, ada_b)

    x_lat = x
    x_ctx = ctx.reshape(1, bsz * n_ctx, d)
    as_ctx = lambda a: a.reshape(bsz, n_ctx, a.shape[-1])
    g64 = _group_matrix([(0, 64), (64, 128)])

    for i in range(depth):
        need_ctx = i < depth - 1
        mod_l = mods[i, :bsz].reshape(bsz, ADA_CHUNKS, d)
        mod_c = mods[i, bsz:bsz + 1].reshape(1, ADA_CHUNKS, d)
        kind, j = i % N_MIXERS, i // N_MIXERS
        gs_l = gs_c = ng = None

        if kind == 0:
            w = ret_w_in[j]
            dk = d // RET_HEADS
            nqk = RET_HEADS * dk
            perm = _tiled_partner(dk, nqk)
            wcat = jnp.concatenate([w, w[:, :nqk][:, perm], w[:, nqk:2 * nqk][:, perm]], axis=1).astype(BF16)
            cos, sin = _rope_cos_sin(seq, dk)
            tab_l = jnp.concatenate([cos, sin], axis=-1)
            tab_c = _identity_tables(bsz * n_ctx, dk)
            widths = [nqk, nqk, 2 * nqk, 2 * nqk]
            ql, kl, vl, gs_l = _pre_call(_pre_ret_kernel, x_lat, mod_l, [tab_l], [wcat], widths, "pre_ret")
            qc, kc, vc, gs_c = _pre_call(_pre_ret_kernel, x_ctx, mod_c, [tab_c], [wcat], widths, "pre_ret_ctx")
            qc, kc, vc = as_ctx(qc), as_ctx(kc), as_ctx(vc)
            lg = jax.nn.log_sigmoid(ret_decay_logit[j].astype(F32))
            o_l = list(_ret_lat_call(lg, ql, kl, vl, kc, vc))
            o_c = [_ret_ctx_call(lg, qc, kc, vc).reshape(1, bsz * n_ctx, -1)] if need_ctx else None
            ng = ret_norm_g[j].reshape(1, -1)
            wo = ret_w_out[j].astype(BF16)
        elif kind == 1:
            w = diff_w_in[j]
            dh = d // (2 * DIFF_HEADS)
            nq = 2 * DIFF_HEADS * dh
            perm = _tiled_partner(dh, nq)
            wcat = jnp.concatenate([w, w[:, :nq][:, perm], w[:, nq:2 * nq][:, perm]], axis=1).astype(BF16)
            cos, sin = _rope_cos_sin(seq, dh)
            tab_l = jnp.concatenate([cos, cos, sin, sin], axis=-1)
            tab_c = _identity_tables(bsz * n_ctx, LANES)
            pl_ = _rope_partner(dh)
            gq = jnp.stack([jnp.tile(diff_q_norm_g[j], 2), jnp.tile(diff_q_norm_g[j][pl_], 2)])
            gk = jnp.stack([jnp.tile(diff_k_norm_g[j], 2), jnp.tile(diff_k_norm_g[j][pl_], 2)])
            kern = functools.partial(_pre_diff_kernel, qscale=dh ** -0.5 * LOG2E)
            widths = [nq, nq, nq, nq]
            qa, qb, kl, vl = _pre_call(kern, x_lat, mod_l, [tab_l], [wcat, gq, gk, g64], widths, "pre_diff")
            qac, qbc, kc, vc = _pre_call(kern, x_ctx, mod_c, [tab_c], [wcat, gq, gk, g64], widths, "pre_diff_ctx")
            qac, qbc, kc, vc = as_ctx(qac), as_ctx(qbc), as_ctx(kc), as_ctx(vc)
            lv = diff_lambda[j].astype(F32)
            lambda_init = 0.8 - 0.6 * math.exp(-0.3 * i)
            lam = (jnp.exp(jnp.sum(lv[0] * lv[1])) - jnp.exp(jnp.sum(lv[2] * lv[3])) + lambda_init).reshape(1)
            sg = diff_subln_g[j].reshape(1, -1)
            o_l = [_diff_flash_call(lam, sg, 1.0 - lambda_init, qa, qb, kc, vc, kl, vl)]
            o_c = ([_diff_flash_call(lam, sg, 1.0 - lambda_init, qac, qbc, kc, vc).reshape(1, bsz * n_ctx, -1)]
                   if need_ctx else None)
            wo = diff_w_out[j].astype(BF16)
        elif kind == 2:
            w = na_w_in[j].astype(BF16)
            dh = d // NA_HEADS
            gq = jnp.tile(na_q_norm_g[j], LANES // dh).reshape(1, LANES)
            gk = jnp.tile(na_k_norm_g[j], LANES // dh).reshape(1, LANES)
            kern = functools.partial(_pre_na_kernel, qscale=dh ** -0.5 * LOG2E)
            widths = [d, d, d]
            ql, kl, vl = _pre_call(kern, x_lat, mod_l, [], [w, gq, gk, g64], widths, "pre_na")
            qc, kc, vc = _pre_call(kern, x_ctx, mod_c, [], [w, gq, gk, g64], widths, "pre_na_ctx")
            qc, kc, vc = as_ctx(qc), as_ctx(kc), as_ctx(vc)
            table = _na_table(na_rel_bias[j].astype(F32), seq // GRID_W)
            o_l = [_na_call(ql, kc, vc, kl, vl, table)]
            o_c = [_na_call(qc, kc, vc).reshape(1, bsz * n_ctx, -1)] if need_ctx else None
            wo = na_w_out[j].astype(BF16)
        else:
            nh = MLA_HEADS
            qk = MLA_NOPE + MLA_ROPE
            wd = mla_w_down[j]
            o2 = MLA_Q_RANK + MLA_KV_RANK
            rope_cols = wd[:, o2:o2 + MLA_ROPE]
            pad = lambda a, lo, hi: jnp.pad(a, ((0, 0), (lo, hi)))
            rope_grp = pad(rope_cols, MLA_NOPE, LANES - qk)
            rope_grp_p = pad(rope_cols[:, _rope_partner(MLA_ROPE)], MLA_NOPE, LANES - qk)
            wdcat = jnp.concatenate([wd[:, :o2], rope_grp, rope_grp_p], axis=1).astype(BF16)

            def head_pad(a, width):
                r = a.shape[0]
                return jnp.pad(a.reshape(r, nh, width), ((0, 0), (0, 0), (0, LANES - width))).reshape(r, nh * LANES)

            wuq = mla_w_uq[j]
            perm = (np.arange(nh * qk) // qk) * qk + np.concatenate(
                [np.arange(MLA_NOPE), MLA_NOPE + _rope_partner(MLA_ROPE)])[np.arange(nh * qk) % qk]
            wuqcat = jnp.concatenate([head_pad(wuq, qk), head_pad(wuq[:, perm], qk)], axis=1).astype(BF16)
            wukv = mla_w_ukv[j].reshape(MLA_KV_RANK, nh, MLA_NOPE + MLA_V)
            wk = head_pad(wukv[:, :, :MLA_NOPE].reshape(MLA_KV_RANK, -1), MLA_NOPE)
            wv = head_pad(wukv[:, :, MLA_NOPE:].reshape(MLA_KV_RANK, -1), MLA_V)
            wukvcat = jnp.concatenate([wk, wv], axis=1).astype(BF16)

            def gain_rows(g):
                gp = jnp.concatenate([g[:MLA_NOPE], g[MLA_NOPE:][_rope_partner(MLA_ROPE)]])
                return jnp.stack([jnp.pad(g, (0, LANES - qk)), jnp.pad(gp, (0, LANES - qk))])

            cos, sin = _rope_cos_sin(seq, MLA_ROPE)
            ones_l = lambda n, w_: jnp.ones((n, w_), F32)
            zeros_l = lambda n, w_: jnp.zeros((n, w_), F32)
            tab_l = jnp.concatenate([ones_l(seq, MLA_NOPE), cos, ones_l(seq, LANES - qk),
                                     zeros_l(seq, MLA_NOPE), sin, zeros_l(seq, LANES - qk)], axis=-1)
            tab_c = _identity_tables(bsz * n_ctx, LANES)
            gmq = _group_matrix([(0, MLA_NOPE), (MLA_NOPE, qk)])
            gmk = _group_matrix([(0, MLA_NOPE)])
            consts = [wdcat, mla_q_norm_g[j].reshape(1, -1), mla_kv_norm_g[j].reshape(1, -1), wuqcat, wukvcat,
                      gain_rows(mla_qk_norm_q[j]), gain_rows(mla_qk_norm_k[j]), gmq, gmk]
            kern = functools.partial(_pre_mla_kernel, qscale=qk ** -0.5 * LOG2E)
            widths = [nh * LANES] * 3
            ql, kl, vl = _pre_call(kern, x_lat, mod_l, [tab_l], consts, widths, "pre_mla")
            qc, kc, vc = _pre_call(kern, x_ctx, mod_c, [tab_c], consts, widths, "pre_mla_ctx")
            qc, kc, vc = as_ctx(qc), as_ctx(kc), as_ctx(vc)
            o_l = [_mla_flash_call(ql, kc, vc, kl, vl)]
            o_c = [_mla_flash_call(qc, kc, vc).reshape(1, bsz * n_ctx, -1)] if need_ctx else None
            wo = jnp.pad(mla_w_out[j].reshape(nh, MLA_V, d), ((0, 0), (0, LANES - MLA_V), (0, 0)))
            wo = wo.reshape(nh * LANES, d).astype(BF16)

        w1 = ffn_w_in[i].astype(BF16)
        w2 = ffn_w_out[i].astype(BF16)
        x_lat = _post_call(o_l, x_lat, mod_l, wo, w1, w2, gs_l, ng, name="post")
        if need_ctx:
            x_ctx = _post_call(o_c, x_ctx, mod_c, wo, w1, w2, gs_c, ng, name="post_ctx")
    return x_lat
```

```python
import functools
import math

import numpy as np
import jax
import jax.numpy as jnp
from jax import lax
from jax.experimental import pallas as pl
from jax.experimental.pallas import tpu as pltpu

F32 = jnp.float32
BF16 = jnp.bfloat16

GRID_W = 64
ROPE_BASE = 10000.0
EPS = 1e-6
ADA_CHUNKS = 6
N_MIXERS = 4
RET_HEADS = 4
RET_CHUNK = 128
DIFF_HEADS = 8
NA_HEADS = 16
NA_WIN_H = 8
NA_WIN_W = 16
MLA_HEADS = 16
MLA_Q_RANK = 256
MLA_KV_RANK = 128
MLA_NOPE = 64
MLA_ROPE = 32
MLA_V = 64

LANES = 128
LOG2E = 1.4426950408889634
NEG = -1e30
V7X_VMEM_BYTES = 64 * 1024 * 1024
VMEM_LIMIT = V7X_VMEM_BYTES - 8 * 1024 * 1024

TM = 256
TQ = 512
TK = 512
NA_ROWS = 4
FLASH_UNROLL = 10

_NT = (((1,), (1,)), ((), ()))
_TN = (((0,), (0,)), ((), ()))


def _cparams(n_axes):
    return pltpu.CompilerParams(dimension_semantics=("arbitrary",) * n_axes,
                                vmem_limit_bytes=VMEM_LIMIT)


def _const_spec(a):
    nd = a.ndim
    return pl.BlockSpec(a.shape, lambda *_: (0,) * nd, pipeline_mode=pl.Buffered(1))


def _sigmoid(v):
    return 1.0 / (1.0 + jnp.exp(-v))


def _modulate(xv, shift, scale):
    ms = jnp.mean(xv * xv, axis=-1, keepdims=True)
    return xv * lax.rsqrt(ms + EPS) * (1.0 + scale) + shift


def _dot(a, b):
    return jnp.dot(a, b, preferred_element_type=F32)


def _group_meansq(z, g):
    z2 = z * z
    hi = z2.astype(BF16)
    lo = (z2 - hi.astype(F32)).astype(BF16)
    return _dot(hi, g) + _dot(lo, g)


def _group_rsqrt(z, g):
    w = g.shape[0]
    parts = [lax.rsqrt(_group_meansq(z[:, c:c + w], g) + EPS) for c in range(0, z.shape[1], w)]
    return jnp.concatenate(parts, axis=1) if len(parts) > 1 else parts[0]


def _ada_kernel(c_ref, w_ref, b_ref, o_ref):
    cv = c_ref[...]
    s = cv * _sigmoid(cv)
    o_ref[0] = _dot(s, w_ref[0]) + b_ref[0]


def _ada_call(cs, ada_w, ada_b):
    depth, d, n = ada_w.shape
    tn = n // 4
    return pl.pallas_call(
        _ada_kernel,
        out_shape=jax.ShapeDtypeStruct((depth, cs.shape[0], n), F32),
        grid=(depth, n // tn),
        in_specs=[pl.BlockSpec(cs.shape, lambda l, j: (0, 0)),
                  pl.BlockSpec((1, d, tn), lambda l, j: (l, 0, j)),
                  pl.BlockSpec((1, 1, tn), lambda l, j: (l, 0, j))],
        out_specs=pl.BlockSpec((1, cs.shape[0], tn), lambda l, j: (l, 0, j)),
        compiler_params=_cparams(2),
        name="ada_mod",
    )(cs, ada_w, ada_b.reshape(depth, 1, n))


def _pre_ret_kernel(x_ref, mod_ref, cs_ref, w_ref, q_ref, k_ref, v_ref, g_ref):
    mod = mod_ref[0]
    h = _modulate(x_ref[0], mod[0:1], mod[1:2]).astype(BF16)
    dk = q_ref.shape[2] // RET_HEADS
    nqk = RET_HEADS * dk
    nv = v_ref.shape[2]
    cos = cs_ref[:, 0:dk]
    sin = cs_ref[:, dk:2 * dk]
    zq = _dot(h, w_ref[:, 0:nqk])
    zk = _dot(h, w_ref[:, nqk:2 * nqk])
    o2 = 2 * nqk + 2 * nv
    zqp = _dot(h, w_ref[:, o2:o2 + nqk])
    zkp = _dot(h, w_ref[:, o2 + nqk:o2 + 2 * nqk])
    kscale = dk ** -0.5
    for hh in range(RET_HEADS):
        sl = slice(hh * dk, (hh + 1) * dk)
        q_ref[0, :, sl] = (zq[:, sl] * cos + zqp[:, sl] * sin).astype(BF16)
        k_ref[0, :, sl] = ((zk[:, sl] * cos + zkp[:, sl] * sin) * kscale).astype(BF16)
    v_ref[0] = _dot(h, w_ref[:, 2 * nqk:2 * nqk + nv]).astype(BF16)
    g = _dot(h, w_ref[:, 2 * nqk + nv:o2])
    g_ref[0] = (g * _sigmoid(g)).astype(BF16)


def _pre_diff_kernel(x_ref, mod_ref, cs_ref, w_ref, gq_ref, gk_ref, gm_ref,
                     qa_ref, qb_ref, k_ref, v_ref, *, qscale):
    mod = mod_ref[0]
    h = _modulate(x_ref[0], mod[0:1], mod[1:2]).astype(BF16)
    n = k_ref.shape[2]
    tm = x_ref.shape[1]
    cos = cs_ref[:, 0:LANES]
    sin = cs_ref[:, LANES:2 * LANES]
    gq = gq_ref[...]
    gk = gk_ref[...]
    qc, qs = gq[0:1] * cos * qscale, gq[1:2] * sin * qscale
    kc, ks = gk[0:1] * cos, gk[1:2] * sin
    gm = gm_ref[...]
    first = lax.broadcasted_iota(jnp.int32, (tm, LANES), 1) < LANES // 2
    zq = _dot(h, w_ref[:, 0:n])
    zk = _dot(h, w_ref[:, n:2 * n])
    zqp = _dot(h, w_ref[:, 3 * n:4 * n])
    zkp = _dot(h, w_ref[:, 4 * n:5 * n])
    rq = _group_rsqrt(zq, gm)
    rk = _group_rsqrt(zk, gm)
    for hh in range(n // LANES):
        sl = slice(hh * LANES, (hh + 1) * LANES)
        q = rq[:, sl] * (zq[:, sl] * qc + zqp[:, sl] * qs)
        qa_ref[0, :, sl] = jnp.where(first, q, 0.0).astype(BF16)
        qb_ref[0, :, sl] = jnp.where(first, 0.0, q).astype(BF16)
        k = rk[:, sl] * (zk[:, sl] * kc + zkp[:, sl] * ks)
        k_ref[0, :, sl] = k.astype(BF16)
    v_ref[0] = _dot(h, w_ref[:, 2 * n:3 * n]).astype(BF16)


def _pre_na_kernel(x_ref, mod_ref, w_ref, gq_ref, gk_ref, gm_ref,
                   q_ref, k_ref, v_ref, *, qscale):
    mod = mod_ref[0]
    h = _modulate(x_ref[0], mod[0:1], mod[1:2]).astype(BF16)
    n = k_ref.shape[2]
    gq = gq_ref[...] * qscale
    gk = gk_ref[...]
    gm = gm_ref[...]
    zq = _dot(h, w_ref[:, 0:n])
    zk = _dot(h, w_ref[:, n:2 * n])
    rq = _group_rsqrt(zq, gm)
    rk = _group_rsqrt(zk, gm)
    for hh in range(n // LANES):
        sl = slice(hh * LANES, (hh + 1) * LANES)
        q_ref[0, :, sl] = (zq[:, sl] * rq[:, sl] * gq).astype(BF16)
        k_ref[0, :, sl] = (zk[:, sl] * rk[:, sl] * gk).astype(BF16)
    v_ref[0] = _dot(h, w_ref[:, 2 * n:3 * n]).astype(BF16)


def _pre_mla_kernel(x_ref, mod_ref, cs_ref, wd_ref, gqr_ref, gkv_ref, wuq_ref, wukv_ref,
                    gq_ref, gk_ref, gmq_ref, gmk_ref, q_ref, k_ref, v_ref, *, qscale):
    mod = mod_ref[0]
    h = _modulate(x_ref[0], mod[0:1], mod[1:2]).astype(BF16)
    tm = x_ref.shape[1]
    n = q_ref.shape[2]
    z = _dot(h, wd_ref[...])
    o1 = MLA_Q_RANK
    o2 = o1 + MLA_KV_RANK
    zq = z[:, 0:o1]
    qn = (zq * lax.rsqrt(jnp.mean(zq * zq, axis=-1, keepdims=True) + EPS) * gqr_ref[...]).astype(BF16)
    zc = z[:, o1:o2]
    cn = (zc * lax.rsqrt(jnp.mean(zc * zc, axis=-1, keepdims=True) + EPS) * gkv_ref[...]).astype(BF16)
    zr = z[:, o2:o2 + LANES]
    zrp = z[:, o2 + LANES:o2 + 2 * LANES]
    cos = cs_ref[:, 0:LANES]
    sin = cs_ref[:, LANES:2 * LANES]
    gq = gq_ref[...]
    gk = gk_ref[...]
    qc, qs = gq[0:1] * cos * qscale, gq[1:2] * sin * qscale
    msr = jnp.sum(zr * zr, axis=-1, keepdims=True) * (1.0 / MLA_ROPE)
    krope = lax.rsqrt(msr + EPS) * (zr * (gk[0:1] * cos) + zrp * (gk[1:2] * sin))
    gmq = gmq_ref[...]
    gmk = gmk_ref[...]
    one_lane = (lax.broadcasted_iota(jnp.int32, (tm, LANES), 1) == MLA_V).astype(F32)
    uq = _dot(qn, wuq_ref[:, 0:n])
    uqp = _dot(qn, wuq_ref[:, n:2 * n])
    uk = _dot(cn, wukv_ref[:, 0:n])
    uv = _dot(cn, wukv_ref[:, n:2 * n])
    rq = _group_rsqrt(uq, gmq)
    rk = _group_rsqrt(uk, gmk)
    for hh in range(n // LANES):
        sl = slice(hh * LANES, (hh + 1) * LANES)
        q = rq[:, sl] * (uq[:, sl] * qc + uqp[:, sl] * qs)
        q_ref[0, :, sl] = q.astype(BF16)
        k = uk[:, sl] * rk[:, sl] * gk[0:1] + krope
        k_ref[0, :, sl] = k.astype(BF16)
        v_ref[0, :, sl] = (uv[:, sl] + one_lane).astype(BF16)


def _pre_call(kern, x, mod, tables, consts, out_widths, name):
    nb, n, d = x.shape
    tm = min(TM, n)
    assert n % tm == 0
    tok = lambda w: pl.BlockSpec((1, tm, w), lambda b, i: (b, i, 0))
    in_specs = [tok(d), pl.BlockSpec((1, ADA_CHUNKS, d), lambda b, i: (b, 0, 0))]
    in_specs += [pl.BlockSpec((tm, t.shape[1]), lambda b, i: (i, 0)) for t in tables]
    in_specs += [_const_spec(a) for a in consts]
    return pl.pallas_call(
        kern,
        out_shape=[jax.ShapeDtypeStruct((nb, n, w), BF16) for w in out_widths],
        grid=(nb, n // tm),
        in_specs=in_specs,
        out_specs=[tok(w) for w in out_widths],
        compiler_params=_cparams(2),
        name=name,
    )(x, mod, *tables, *consts)


def _scalar_vec(s):
    return jnp.full((1, 1), s, F32)


def _ret_lat_kernel(lg_ref, qf_ref, kf_ref, vf_ref, qb_ref, kb_ref, vb_ref, kc_ref, vc_ref,
                    of_ref, ob_ref, sf_s, sb_s):
    c = pl.program_id(1)
    cn = qf_ref.shape[1]
    n_ctx = kc_ref.shape[1]
    dk = qf_ref.shape[2] // RET_HEADS
    dv = vf_ref.shape[2] // RET_HEADS

    @pl.when(c == 0)
    def _():
        pos = lax.broadcasted_iota(jnp.int32, (n_ctx, 1), 0).astype(F32)
        for hh in range(RET_HEADS):
            lf = _scalar_vec(lg_ref[0, hh])
            lb = _scalar_vec(lg_ref[1, hh])
            kc = kc_ref[0, :, hh * dk:(hh + 1) * dk].astype(F32)
            vc = vc_ref[0, :, hh * dv:(hh + 1) * dv]
            wf = jnp.exp((n_ctx - 1.0 - pos) * lf)
            wb = jnp.exp(pos * lb)
            sf_s[hh] = lax.dot_general((kc * wf).astype(BF16), vc, _TN, preferred_element_type=F32)
            sb_s[hh] = lax.dot_general((kc * wb).astype(BF16), vc, _TN, preferred_element_type=F32)

    ii = lax.broadcasted_iota(jnp.int32, (cn, cn), 0)
    jj = lax.broadcasted_iota(jnp.int32, (cn, cn), 1)
    rel = (ii - jj).astype(F32)
    idx = lax.broadcasted_iota(jnp.int32, (cn, 1), 0).astype(F32)

    for hh in range(RET_HEADS):
        lf = _scalar_vec(lg_ref[0, hh])
        lb = _scalar_vec(lg_ref[1, hh])
        ksl = slice(hh * dk, (hh + 1) * dk)
        vsl = slice(hh * dv, (hh + 1) * dv)

        q, k, v = qf_ref[0, :, ksl], kf_ref[0, :, ksl], vf_ref[0, :, vsl]
        dec = jnp.where(rel >= 0, jnp.exp(jnp.maximum(rel, 0.0) * lf), 0.0)
        att = lax.dot_general(q, k, _NT, preferred_element_type=F32) * dec
        s = sf_s[hh]
        of_ref[0, :, vsl] = _dot(att.astype(BF16), v) + _dot(q, s.astype(BF16)) * jnp.exp((idx + 1.0) * lf)
        kd = (k.astype(F32) * jnp.exp((cn - 1.0 - idx) * lf)).astype(BF16)
        sf_s[hh] = s * jnp.exp(cn * lf) + lax.dot_general(kd, v, _TN, preferred_element_type=F32)

        q, k, v = qb_ref[0, :, ksl], kb_ref[0, :, ksl], vb_ref[0, :, vsl]
        dec = jnp.where(rel <= 0, jnp.exp(jnp.maximum(-rel, 0.0) * lb), 0.0)
        att = lax.dot_general(q, k, _NT, preferred_element_type=F32) * dec
        s = sb_s[hh]
        ob_ref[0, :, vsl] = _dot(att.astype(BF16), v) + _dot(q, s.astype(BF16)) * jnp.exp((cn - idx) * lb)
        kd = (k.astype(F32) * jnp.exp(idx * lb)).astype(BF16)
        sb_s[hh] = s * jnp.exp(cn * lb) + lax.dot_general(kd, v, _TN, preferred_element_type=F32)


def _ret_lat_call(lg, q, k, v, kc, vc):
    b, n, nqk = q.shape
    dk = nqk // RET_HEADS
    nv = v.shape[2]
    dv = nv // RET_HEADS
    n_ctx = kc.shape[1]
    cn = RET_CHUNK
    nc = n // cn
    fwd = lambda w: pl.BlockSpec((1, cn, w), lambda bb, c: (bb, c, 0))
    bwd = lambda w: pl.BlockSpec((1, cn, w), lambda bb, c: (bb, nc - 1 - c, 0))
    ctx = lambda w: pl.BlockSpec((1, n_ctx, w), lambda bb, c: (bb, 0, 0))
    return pl.pallas_call(
        _ret_lat_kernel,
        out_shape=[jax.ShapeDtypeStruct((b, n, nv), F32)] * 2,
        grid=(b, nc),
        in_specs=[pl.BlockSpec(memory_space=pltpu.SMEM),
                  fwd(nqk), fwd(nqk), fwd(nv), bwd(nqk), bwd(nqk), bwd(nv), ctx(nqk), ctx(nv)],
        out_specs=[fwd(nv), bwd(nv)],
        scratch_shapes=[pltpu.VMEM((RET_HEADS, dk, dv), F32), pltpu.VMEM((RET_HEADS, dk, dv), F32)],
        compiler_params=_cparams(2),
        name="ret_lat",
    )(lg, q, k, v, q, k, v, kc, vc)


def _ret_ctx_kernel(lg_ref, q_ref, k_ref, v_ref, o_ref):
    hh = pl.program_id(1)
    lf = _scalar_vec(lg_ref[0, hh])
    lb = _scalar_vec(lg_ref[1, hh])
    n = q_ref.shape[1]
    ii = lax.broadcasted_iota(jnp.int32, (n, n), 0)
    jj = lax.broadcasted_iota(jnp.int32, (n, n), 1)
    rel = (ii - jj).astype(F32)
    dec = (jnp.where(rel >= 0, jnp.exp(jnp.maximum(rel, 0.0) * lf), 0.0)
           + jnp.where(rel <= 0, jnp.exp(jnp.maximum(-rel, 0.0) * lb), 0.0))
    att = lax.dot_general(q_ref[0], k_ref[0], _NT, preferred_element_type=F32) * dec
    o_ref[0] = _dot(att.astype(BF16), v_ref[0])


def _ret_ctx_call(lg, q, k, v):
    b, n, nqk = q.shape
    dk = nqk // RET_HEADS
    dv = v.shape[2] // RET_HEADS
    spec = lambda w: pl.BlockSpec((1, n, w), lambda bb, hh: (bb, 0, hh))
    return pl.pallas_call(
        _ret_ctx_kernel,
        out_shape=jax.ShapeDtypeStruct((b, n, RET_HEADS * dv), F32),
        grid=(b, RET_HEADS),
        in_specs=[pl.BlockSpec(memory_space=pltpu.SMEM), spec(dk), spec(dk), spec(dv)],
        out_specs=spec(dv),
        compiler_params=_cparams(2),
        name="ret_ctx",
    )(lg, q, k, v)


def _lane_tiles(v, n):
    return jnp.concatenate([v] * n, axis=1) if n > 1 else v


def _softmax_update(s, m_s, l_s):
    n = s.shape[1] // LANES
    m_prev = m_s[...]
    m_new = jnp.maximum(m_prev, jnp.max(s, axis=-1, keepdims=True))
    alpha = jnp.exp2(m_prev - m_new)
    m_s[...] = m_new
    p = jnp.exp2(s - _lane_tiles(m_new, n))
    if l_s is not None:
        part = p[:, 0:LANES]
        for t in range(1, n):
            part = part + p[:, t * LANES:(t + 1) * LANES]
        l_s[...] = alpha * l_s[...] + part
    return alpha, p


def _flash_streams(streams, kl_ref, vl_ref, kc_ref, vc_ref, n_main):
    for q, m_s, l_s, acc_s, _, _, _ in streams:
        m_s[...] = jnp.full(m_s.shape, NEG, F32)
        acc_s[...] = jnp.zeros(acc_s.shape, F32)
        if l_s is not None:
            l_s[...] = jnp.zeros(l_s.shape, F32)
        s = lax.dot_general(q, kc_ref[0], _NT, preferred_element_type=F32)
        alpha, p = _softmax_update(s, m_s, l_s)
        acc_s[...] = alpha * acc_s[...] + _dot(p.astype(BF16), vc_ref[0])
    if not n_main:
        return
    assert n_main >= 2 and n_main % 2 == 0

    def chunk(ref, j):
        off = j * TK if isinstance(j, int) else pl.multiple_of(j * TK, TK)
        return ref[0, pl.ds(off, TK), :]

    def scores(j, slot):
        k_c = chunk(kl_ref, j)
        for q, _, _, _, s_scr, _, _ in streams:
            s_scr[slot] = lax.dot_general(q, k_c, _NT, preferred_element_type=F32)

    def softmax(slot):
        for _, m_s, l_s, _, s_scr, p_scr, al_scr in streams:
            alpha, p = _softmax_update(s_scr[slot], m_s, l_s)
            al_scr[slot] = alpha
            p_scr[slot] = p.astype(BF16)

    def values(j, slot):
        v_c = chunk(vl_ref, j)
        for _, _, _, acc_s, _, p_scr, al_scr in streams:
            acc_s[...] = al_scr[slot] * acc_s[...] + _dot(p_scr[slot], v_c)

    scores(0, 0)
    scores(1, 1)
    softmax(0)

    def step(j, par):
        scores(j + 2, par)
        softmax(1 - par)
        values(j, par)

    n_pipe = n_main - 2
    n_loop = n_pipe // FLASH_UNROLL

    def body(t, carry):
        for u in range(FLASH_UNROLL):
            step(t * FLASH_UNROLL + u, u % 2)
        return carry

    if n_loop:
        lax.fori_loop(0, n_loop, body, 0)
    for j in range(n_loop * FLASH_UNROLL, n_pipe):
        step(j, j % 2)
    softmax(1)
    values(n_main - 2, 0)
    values(n_main - 1, 1)


def _stream_scratch(tq, n_main, with_l):
    stat = [pltpu.VMEM((tq, LANES), F32)] * (3 if with_l else 2)
    if not n_main:
        return stat
    return stat + [pltpu.VMEM((2, tq, TK), F32), pltpu.VMEM((2, tq, TK), BF16), pltpu.VMEM((2, tq, LANES), F32)]


def _mla_flash_kernel(*refs, n_main):
    if n_main:
        q_ref, kl_ref, vl_ref, kc_ref, vc_ref, o_ref, m_s, acc_s, s_scr, p_scr, al_scr = refs
    else:
        q_ref, kc_ref, vc_ref, o_ref, m_s, acc_s = refs
        kl_ref = vl_ref = s_scr = p_scr = al_scr = None
    _flash_streams([(q_ref[0], m_s, None, acc_s, s_scr, p_scr, al_scr)], kl_ref, vl_ref, kc_ref, vc_ref, n_main)
    acc = acc_s[...]
    o_ref[0] = (acc / acc[:, MLA_V:MLA_V + 1]).astype(BF16)


def _mla_flash_call(q, k_ctx, v_ctx, k_lat=None, v_lat=None):
    b, nq, w = q.shape
    nh = w // LANES
    n_ctx = k_ctx.shape[1]
    tq = min(TQ, nq)
    assert nq % tq == 0
    qspec = pl.BlockSpec((1, tq, LANES), lambda bb, hh, i: (bb, i, hh))
    full = lambda n: pl.BlockSpec((1, n, LANES), lambda bb, hh, i: (bb, 0, hh))
    args, in_specs, n_main = [q], [qspec], 0
    if k_lat is not None:
        n_lat = k_lat.shape[1]
        assert n_lat % TK == 0
        n_main = n_lat // TK
        args += [k_lat, v_lat]
        in_specs += [full(n_lat), full(n_lat)]
    args += [k_ctx, v_ctx]
    in_specs += [full(n_ctx), full(n_ctx)]
    return pl.pallas_call(
        functools.partial(_mla_flash_kernel, n_main=n_main),
        out_shape=jax.ShapeDtypeStruct((b, nq, w), BF16),
        grid=(b, nh, nq // tq),
        in_specs=in_specs,
        out_specs=qspec,
        scratch_shapes=_stream_scratch(tq, n_main, with_l=False),
        compiler_params=_cparams(3),
        name="mla_attn" if n_main else "mla_attn_ctx",
    )(*args)


def _diff_flash_kernel(*refs, n_main, out_scale):
    if n_main:
        (lam_ref, qa_ref, qb_ref, kl_ref, vl_ref, kc_ref, vc_ref, g_ref, o_ref,
         m1, l1, a1, m2, l2, a2, s_scr, p_scr, al_scr) = refs
    else:
        (lam_ref, qa_ref, qb_ref, kc_ref, vc_ref, g_ref, o_ref, m1, l1, a1, m2, l2, a2) = refs
        kl_ref = vl_ref = s_scr = p_scr = al_scr = None
    for q_ref, m_s, l_s, a_s in ((qa_ref, m1, l1, a1), (qb_ref, m2, l2, a2)):
        _flash_streams([(q_ref[0], m_s, l_s, a_s, s_scr, p_scr, al_scr)], kl_ref, vl_ref, kc_ref, vc_ref, n_main)
    rowsum = lambda l_s: jnp.sum(l_s[...], axis=-1, keepdims=True)
    o = a1[...] / rowsum(l1) - lam_ref[0] * (a2[...] / rowsum(l2))
    ms = jnp.mean(o * o, axis=-1, keepdims=True)
    o_ref[0] = (o * lax.rsqrt(ms + EPS) * (g_ref[...] * out_scale)).astype(BF16)


def _diff_flash_call(lam, subln_g, out_scale, qa, qb, k_ctx, v_ctx, k_lat=None, v_lat=None):
    b, nq, w = qa.shape
    nh = w // LANES
    n_ctx = k_ctx.shape[1]
    tq = min(TQ, nq)
    assert nq % tq == 0
    qspec = pl.BlockSpec((1, tq, LANES), lambda bb, hh, i: (bb, i, hh))
    full = lambda n: pl.BlockSpec((1, n, LANES), lambda bb, hh, i: (bb, 0, hh))
    args = [lam, qa, qb]
    in_specs = [pl.BlockSpec(memory_space=pltpu.SMEM), qspec, qspec]
    n_main = 0
    if k_lat is not None:
        n_lat = k_lat.shape[1]
        assert n_lat % TK == 0
        n_main = n_lat // TK
        args += [k_lat, v_lat]
        in_specs += [full(n_lat), full(n_lat)]
    args += [k_ctx, v_ctx, subln_g]
    in_specs += [full(n_ctx), full(n_ctx), pl.BlockSpec((1, LANES), lambda bb, hh, i: (0, 0))]
    stat = _stream_scratch(tq, 0, with_l=True)
    stage = _stream_scratch(tq, n_main, with_l=True)[len(stat):]
    return pl.pallas_call(
        functools.partial(_diff_flash_kernel, n_main=n_main, out_scale=out_scale),
        out_shape=jax.ShapeDtypeStruct((b, nq, w), BF16),
        grid=(b, nh, nq // tq),
        in_specs=in_specs,
        out_specs=qspec,
        scratch_shapes=stat + stat + stage,
        compiler_params=_cparams(3),
        name="diff_attn" if n_main else "diff_attn_ctx",
    )(*args)


def _na_kernel(*refs, n_loc):
    q_ref = refs[0]
    k_loc = refs[1:1 + n_loc]
    v_loc = refs[1 + n_loc:1 + 2 * n_loc]
    rest = refs[1 + 2 * n_loc:]
    if n_loc:
        tab_ref, kc_ref, vc_ref, o_ref = rest
    else:
        kc_ref, vc_ref, o_ref = rest
    q = q_ref[0]
    tq = q.shape[0]
    first = lax.broadcasted_iota(jnp.int32, (tq, LANES), 1) < LANES // 2
    zero = jnp.zeros_like(q)
    outs = []
    for hh in range(2):
        qh = jnp.where(first, q, zero) if hh == 0 else jnp.where(first, zero, q)
        ss = []
        for j in range(n_loc):
            tkb = k_loc[j].shape[1]
            s = lax.dot_general(qh, k_loc[j][0], _NT, preferred_element_type=F32)
            ss.append(s + tab_ref[0, hh, :, j * tkb:(j + 1) * tkb])
        ss.append(lax.dot_general(qh, kc_ref[0], _NT, preferred_element_type=F32))
        m = functools.reduce(jnp.maximum, [jnp.max(s, axis=-1, keepdims=True) for s in ss])
        ps = [jnp.exp2(s - m) for s in ss]
        l = functools.reduce(lambda a, c: a + c, [jnp.sum(p, axis=-1, keepdims=True) for p in ps])
        vs = [r[0] for r in v_loc] + [vc_ref[0]]
        o = functools.reduce(lambda a, c: a + c, [_dot(p.astype(BF16), v) for p, v in zip(ps, vs)])
        outs.append(o / l)
    o_ref[0] = jnp.where(first, outs[0], outs[1]).astype(BF16)


def _na_call(q, k_ctx, v_ctx, k_lat=None, v_lat=None, table=None):
    b, nq, w = q.shape
    ng = w // LANES
    n_ctx = k_ctx.shape[1]
    if k_lat is None:
        tq, nblk, n_loc = nq, 1, 0
    else:
        tq = NA_ROWS * GRID_W
        nblk = nq // tq
        n_loc = 3
        assert nq % tq == 0 and nblk >= 3
    qspec = pl.BlockSpec((1, tq, LANES), lambda g, bb, i: (bb, i, g))
    prev = pl.BlockSpec((1, tq, LANES), lambda g, bb, i: (bb, jnp.maximum(i - 1, 0), g))
    nxt = pl.BlockSpec((1, tq, LANES), lambda g, bb, i: (bb, jnp.minimum(i + 1, nblk - 1), g))
    cspec = pl.BlockSpec((1, n_ctx, LANES), lambda g, bb, i: (bb, 0, g))
    args, in_specs = [q], [qspec]
    if n_loc:
        args += [k_lat] * 3 + [v_lat] * 3 + [table]
        variant = lambda i: jnp.where(i == 0, 0, jnp.where(i == nblk - 1, 2, 1))
        in_specs += [prev, qspec, nxt] * 2
        in_specs += [pl.BlockSpec((1, 2, tq, 3 * tq), lambda g, bb, i: (variant(i), g, 0, 0))]
    args += [k_ctx, v_ctx]
    in_specs += [cspec, cspec]
    return pl.pallas_call(
        functools.partial(_na_kernel, n_loc=n_loc),
        out_shape=jax.ShapeDtypeStruct((b, nq, w), BF16),
        grid=(ng, b, nblk),
        in_specs=in_specs,
        out_specs=qspec,
        compiler_params=_cparams(3),
        name="na_attn" if n_loc else "na_attn_ctx",
    )(*args)


def _na_table(rel_bias, rows):
    r_blk = NA_ROWS
    nblk = rows // r_blk
    wh = min(NA_WIN_H, rows)
    col = np.arange(GRID_W)
    c0 = np.clip(col - NA_WIN_W // 2, 0, GRID_W - NA_WIN_W)
    dcol = col[None, :] - col[:, None]
    col_ok = (col[None, :] >= c0[:, None]) & (col[None, :] < c0[:, None] + NA_WIN_W)
    dc_idx = np.clip(dcol + NA_WIN_W - 1, 0, 2 * NA_WIN_W - 2)
    nh = rel_bias.shape[0]
    t1 = jnp.where(col_ok[None, None], rel_bias[:, :, dc_idx] * LOG2E, NEG)
    t1 = jnp.concatenate([t1, jnp.full((nh, 1, GRID_W, GRID_W), NEG, F32)], axis=1)
    masked = 2 * NA_WIN_H - 1
    idx = np.full((3, r_blk, 3 * r_blk), masked, np.int32)
    for v, i in enumerate((0, 1, nblk - 1)):
        for qr in range(r_blk):
            r = i * r_blk + qr
            r0 = min(max(r - NA_WIN_H // 2, 0), rows - wh)
            for slot, blk in enumerate((i - 1, i, i + 1)):
                if blk < 0 or blk >= nblk:
                    continue
                for kr_l in range(r_blk):
                    kr = blk * r_blk + kr_l
                    if r0 <= kr < r0 + wh:
                        idx[v, qr, slot * r_blk + kr_l] = kr - r + NA_WIN_H - 1
    tab = t1[:, idx]
    tab = jnp.transpose(tab, (1, 0, 2, 4, 3, 5))
    return tab.reshape(3, nh, r_blk * GRID_W, 3 * r_blk * GRID_W)


def _post_kernel(*refs, n_o, ret):
    o_refs = refs[:n_o]
    refs = refs[n_o:]
    if ret:
        gs_ref, ng_ref = refs[:2]
        refs = refs[2:]
    x_ref, mod_ref, wo_ref, w1_ref, w2_ref, out_ref = refs
    mod = mod_ref[0]
    if ret:
        o = functools.reduce(lambda a, c: a + c, [r[0] for r in o_refs])
        dv = o.shape[1] // RET_HEADS
        y = None
        for hh in range(RET_HEADS):
            sl = slice(hh * dv, (hh + 1) * dv)
            oh = o[:, sl]
            d = oh - jnp.mean(oh, axis=-1, keepdims=True)
            var = jnp.mean(d * d, axis=-1, keepdims=True)
            on = d * lax.rsqrt(var + EPS) * ng_ref[:, sl] * gs_ref[0, :, sl].astype(F32)
            part = _dot(on.astype(BF16), wo_ref[sl, :])
            y = part if y is None else y + part
    else:
        y = _dot(o_refs[0][0], wo_ref[...])
    x1 = x_ref[0] + mod[2:3] * y
    h2 = _modulate(x1, mod[3:4], mod[4:5]).astype(BF16)
    fh = w2_ref.shape[0]
    ua = _dot(h2, w1_ref[:, 0:fh])
    ug = _dot(h2, w1_ref[:, fh:2 * fh])
    act = (ua * _sigmoid(ua) * ug).astype(BF16)
    out_ref[0] = x1 + mod[5:6] * _dot(act, w2_ref[...])


def _post_call(os_, x, mod, wo, w1, w2, gs=None, ng=None, name="post"):
    nb, n, d = x.shape
    tm = min(TM, n)
    assert n % tm == 0
    tok = lambda w: pl.BlockSpec((1, tm, w), lambda b, i: (b, i, 0))
    ret = gs is not None
    args = list(os_)
    in_specs = [tok(o.shape[2]) for o in os_]
    if ret:
        args += [gs, ng]
        in_specs += [tok(gs.shape[2]), _const_spec(ng)]
    args += [x, mod, wo, w1, w2]
    in_specs += [tok(d), pl.BlockSpec((1, ADA_CHUNKS, d), lambda b, i: (b, 0, 0)),
                 _const_spec(wo), _const_spec(w1), _const_spec(w2)]
    return pl.pallas_call(
        functools.partial(_post_kernel, n_o=len(os_), ret=ret),
        out_shape=jax.ShapeDtypeStruct((nb, n, d), F32),
        grid=(nb, n // tm),
        in_specs=in_specs,
        out_specs=tok(d),
        compiler_params=_cparams(2),
        name=name,
    )(*args)


def _rope_cos_sin(n, d):
    quarter = d // 4
    pos = jnp.arange(n)
    row = (pos // GRID_W).astype(F32)
    col = (pos % GRID_W).astype(F32)
    inv_freq = jnp.power(ROPE_BASE, -jnp.arange(quarter, dtype=F32) / quarter)
    ar = row[:, None] * inv_freq[None, :]
    ac = col[:, None] * inv_freq[None, :]
    cos = jnp.concatenate([jnp.cos(ar), jnp.cos(ar), jnp.cos(ac), jnp.cos(ac)], axis=-1)
    sin = jnp.concatenate([-jnp.sin(ar), jnp.sin(ar), -jnp.sin(ac), jnp.sin(ac)], axis=-1)
    return cos, sin


def _rope_partner(d):
    half, quarter = d // 2, d // 4
    l = np.arange(d)
    return np.where((l % half) < quarter, l + quarter, l - quarter)


def _tiled_partner(d, n):
    return (np.arange(n) // d) * d + _rope_partner(d)[np.arange(n) % d]


V7X_MXU_WIDTH = 256


def _group_matrix(blocks):
    g = np.zeros((V7X_MXU_WIDTH, V7X_MXU_WIDTH), np.float32)
    for base in range(0, V7X_MXU_WIDTH, LANES):
        for a, b in blocks:
            g[base + a:base + b, base + a:base + b] = 1.0 / (b - a)
    return jnp.asarray(g, BF16)


def _identity_tables(n, w):
    return jnp.concatenate([jnp.ones((n, w), F32), jnp.zeros((n, w), F32)], axis=-1)


def kernel(x, c, ctx, c_ctx, ada_w, ada_b, ret_w_in, ret_decay_logit, ret_norm_g, ret_w_out, diff_w_in, diff_q_norm_g, diff_k_norm_g, diff_lambda, diff_subln_g, diff_w_out, na_w_in, na_q_norm_g, na_k_norm_g, na_rel_bias, na_w_out, mla_w_down, mla_q_norm_g, mla_kv_norm_g, mla_w_uq, mla_w_ukv, mla_qk_norm_q, mla_qk_norm_k, mla_w_out, ffn_w_in, ffn_w_out):
    bsz, seq, d = x.shape
    n_ctx = ctx.shape[1]
    depth = ada_w.shape[0]
    assert seq % GRID_W == 0 and bsz + 1 <= 8

    cs = jnp.zeros((8, d), F32).at[:bsz].set(c).at[bsz].set(c_ctx)
    mods = _ada_call(cs, ada_w, ada_b)

    x_lat = x
    x_ctx = ctx.reshape(1, bsz * n_ctx, d)
    as_ctx = lambda a: a.reshape(bsz, n_ctx, a.shape[-1])
    g64 = _group_matrix([(0, 64), (64, 128)])

    for i in range(depth):
        need_ctx = i < depth - 1
        mod_l = mods[i, :bsz].reshape(bsz, ADA_CHUNKS, d)
        mod_c = mods[i, bsz:bsz + 1].reshape(1, ADA_CHUNKS, d)
        kind, j = i % N_MIXERS, i // N_MIXERS
        gs_l = gs_c = ng = None

        if kind == 0:
            w = ret_w_in[j]
            dk = d // RET_HEADS
            nqk = RET_HEADS * dk
            perm = _tiled_partner(dk, nqk)
            wcat = jnp.concatenate([w, w[:, :nqk][:, perm], w[:, nqk:2 * nqk][:, perm]], axis=1).astype(BF16)
            cos, sin = _rope_cos_sin(seq, dk)
            tab_l = jnp.concatenate([cos, sin], axis=-1)
            tab_c = _identity_tables(bsz * n_ctx, dk)
            widths = [nqk, nqk, 2 * nqk, 2 * nqk]
            ql, kl, vl, gs_l = _pre_call(_pre_ret_kernel, x_lat, mod_l, [tab_l], [wcat], widths, "pre_ret")
            qc, kc, vc, gs_c = _pre_call(_pre_ret_kernel, x_ctx, mod_c, [tab_c], [wcat], widths, "pre_ret_ctx")
            qc, kc, vc = as_ctx(qc), as_ctx(kc), as_ctx(vc)
            lg = jax.nn.log_sigmoid(ret_decay_logit[j].astype(F32))
            o_l = list(_ret_lat_call(lg, ql, kl, vl, kc, vc))
            o_c = [_ret_ctx_call(lg, qc, kc, vc).reshape(1, bsz * n_ctx, -1)] if need_ctx else None
            ng = ret_norm_g[j].reshape(1, -1)
            wo = ret_w_out[j].astype(BF16)
        elif kind == 1:
            w = diff_w_in[j]
            dh = d // (2 * DIFF_HEADS)
            nq = 2 * DIFF_HEADS * dh
            perm = _tiled_partner(dh, nq)
            wcat = jnp.concatenate([w, w[:, :nq][:, perm], w[:, nq:2 * nq][:, perm]], axis=1).astype(BF16)
            cos, sin = _rope_cos_sin(seq, dh)
            tab_l = jnp.concatenate([cos, cos, sin, sin], axis=-1)
            tab_c = _identity_tables(bsz * n_ctx, LANES)
            pl_ = _rope_partner(dh)
            gq = jnp.stack([jnp.tile(diff_q_norm_g[j], 2), jnp.tile(diff_q_norm_g[j][pl_], 2)])
            gk = jnp.stack([jnp.tile(diff_k_norm_g[j], 2), jnp.tile(diff_k_norm_g[j][pl_], 2)])
            kern = functools.partial(_pre_diff_kernel, qscale=dh ** -0.5 * LOG2E)
            widths = [nq, nq, nq, nq]
            qa, qb, kl, vl = _pre_call(kern, x_lat, mod_l, [tab_l], [wcat, gq, gk, g64], widths, "pre_diff")
            qac, qbc, kc, vc = _pre_call(kern, x_ctx, mod_c, [tab_c], [wcat, gq, gk, g64], widths, "pre_diff_ctx")
            qac, qbc, kc, vc = as_ctx(qac), as_ctx(qbc), as_ctx(kc), as_ctx(vc)
            lv = diff_lambda[j].astype(F32)
            lambda_init = 0.8 - 0.6 * math.exp(-0.3 * i)
            lam = (jnp.exp(jnp.sum(lv[0] * lv[1])) - jnp.exp(jnp.sum(lv[2] * lv[3])) + lambda_init).reshape(1)
            sg = diff_subln_g[j].reshape(1, -1)
            o_l = [_diff_flash_call(lam, sg, 1.0 - lambda_init, qa, qb, kc, vc, kl, vl)]
            o_c = ([_diff_flash_call(lam, sg, 1.0 - lambda_init, qac, qbc, kc, vc).reshape(1, bsz * n_ctx, -1)]
                   if need_ctx else None)
            wo = diff_w_out[j].astype(BF16)
        elif kind == 2:
            w = na_w_in[j].astype(BF16)
            dh = d // NA_HEADS
            gq = jnp.tile(na_q_norm_g[j], LANES // dh).reshape(1, LANES)
            gk = jnp.tile(na_k_norm_g[j], LANES // dh).reshape(1, LANES)
            kern = functools.partial(_pre_na_kernel, qscale=dh ** -0.5 * LOG2E)
            widths = [d, d, d]
            ql, kl, vl = _pre_call(kern, x_lat, mod_l, [], [w, gq, gk, g64], widths, "pre_na")
            qc, kc, vc = _pre_call(kern, x_ctx, mod_c, [], [w, gq, gk, g64], widths, "pre_na_ctx")
            qc, kc, vc = as_ctx(qc), as_ctx(kc), as_ctx(vc)
            table = _na_table(na_rel_bias[j].astype(F32), seq // GRID_W)
            o_l = [_na_call(ql, kc, vc, kl, vl, table)]
            o_c = [_na_call(qc, kc, vc).reshape(1, bsz * n_ctx, -1)] if need_ctx else None
            wo = na_w_out[j].astype(BF16)
        else:
            nh = MLA_HEADS
            qk = MLA_NOPE + MLA_ROPE
            wd = mla_w_down[j]
            o2 = MLA_Q_RANK + MLA_KV_RANK
            rope_cols = wd[:, o2:o2 + MLA_ROPE]
            pad = lambda a, lo, hi: jnp.pad(a, ((0, 0), (lo, hi)))
            rope_grp = pad(rope_cols, MLA_NOPE, LANES - qk)
            rope_grp_p = pad(rope_cols[:, _rope_partner(MLA_ROPE)], MLA_NOPE, LANES - qk)
            wdcat = jnp.concatenate([wd[:, :o2], rope_grp, rope_grp_p], axis=1).astype(BF16)

            def head_pad(a, width):
                r = a.shape[0]
                return jnp.pad(a.reshape(r, nh, width), ((0, 0), (0, 0), (0, LANES - width))).reshape(r, nh * LANES)

            wuq = mla_w_uq[j]
            perm = (np.arange(nh * qk) // qk) * qk + np.concatenate(
                [np.arange(MLA_NOPE), MLA_NOPE + _rope_partner(MLA_ROPE)])[np.arange(nh * qk) % qk]
            wuqcat = jnp.concatenate([head_pad(wuq, qk), head_pad(wuq[:, perm], qk)], axis=1).astype(BF16)
            wukv = mla_w_ukv[j].reshape(MLA_KV_RANK, nh, MLA_NOPE + MLA_V)
            wk = head_pad(wukv[:, :, :MLA_NOPE].reshape(MLA_KV_RANK, -1), MLA_NOPE)
            wv = head_pad(wukv[:, :, MLA_NOPE:].reshape(MLA_KV_RANK, -1), MLA_V)
            wukvcat = jnp.concatenate([wk, wv], axis=1).astype(BF16)

            def gain_rows(g):
                gp = jnp.concatenate([g[:MLA_NOPE], g[MLA_NOPE:][_rope_partner(MLA_ROPE)]])
                return jnp.stack([jnp.pad(g, (0, LANES - qk)), jnp.pad(gp, (0, LANES - qk))])

            cos, sin = _rope_cos_sin(seq, MLA_ROPE)
            ones_l = lambda n, w_: jnp.ones((n, w_), F32)
            zeros_l = lambda n, w_: jnp.zeros((n, w_), F32)
            tab_l = jnp.concatenate([ones_l(seq, MLA_NOPE), cos, ones_l(seq, LANES - qk),
                                     zeros_l(seq, MLA_NOPE), sin, zeros_l(seq, LANES - qk)], axis=-1)
            tab_c = _identity_tables(bsz * n_ctx, LANES)
            gmq = _group_matrix([(0, MLA_NOPE), (MLA_NOPE, qk)])
            gmk = _group_matrix([(0, MLA_NOPE)])
            consts = [wdcat, mla_q_norm_g[j].reshape(1, -1), mla_kv_norm_g[j].reshape(1, -1), wuqcat, wukvcat,
                      gain_rows(mla_qk_norm_q[j]), gain_rows(mla_qk_norm_k[j]), gmq, gmk]
            kern = functools.partial(_pre_mla_kernel, qscale=qk ** -0.5 * LOG2E)
            widths = [nh * LANES] * 3
            ql, kl, vl = _pre_call(kern, x_lat, mod_l, [tab_l], consts, widths, "pre_mla")
            qc, kc, vc = _pre_call(kern, x_ctx, mod_c, [tab_c], consts, widths, "pre_mla_ctx")
            qc, kc, vc = as_ctx(qc), as_ctx(kc), as_ctx(vc)
            o_l = [_mla_flash_call(ql, kc, vc, kl, vl)]
            o_c = [_mla_flash_call(qc, kc, vc).reshape(1, bsz * n_ctx, -1)] if need_ctx else None
            wo = jnp.pad(mla_w_out[j].reshape(nh, MLA_V, d), ((0, 0), (0, LANES - MLA_V), (0, 0)))
            wo = wo.reshape(nh * LANES, d).astype(BF16)

        w1 = ffn_w_in[i].astype(BF16)
        w2 = ffn_w_out[i].astype(BF16)
        x_lat = _post_call(o_l, x_lat, mod_l, wo, w1, w2, gs_l, ng, name="post")
        if need_ctx:
            x_ctx = _post_call(o_c, x_ctx, mod_c, wo, w1, w2, gs_c, ng, name="post_ctx")
    return x_lat
```

```python
import functools
import math

import numpy as np
import jax
import jax.numpy as jnp
from jax import lax
from jax.experimental import pallas as pl
from jax.experimental.pallas import tpu as pltpu

F32 = jnp.float32
BF16 = jnp.bfloat16

GRID_W = 64
ROPE_BASE = 10000.0
EPS = 1e-6
ADA_CHUNKS = 6
N_MIXERS = 4
RET_HEADS = 4
RET_CHUNK = 128
DIFF_HEADS = 8
NA_HEADS = 16
NA_WIN_H = 8
NA_WIN_W = 16
MLA_HEADS = 16
MLA_Q_RANK = 256
MLA_KV_RANK = 128
MLA_NOPE = 64
MLA_ROPE = 32
MLA_V = 64

LANES = 128
LOG2E = 1.4426950408889634
NEG = -1e30
V7X_VMEM_BYTES = 64 * 1024 * 1024
VMEM_LIMIT = V7X_VMEM_BYTES - 8 * 1024 * 1024

TM = 256
TQ = 512
TK = 512
NA_ROWS = 4
FLASH_UNROLL = 10

_NT = (((1,), (1,)), ((), ()))
_TN = (((0,), (0,)), ((), ()))


def _cparams(n_axes):
    return pltpu.CompilerParams(dimension_semantics=("arbitrary",) * n_axes,
                                vmem_limit_bytes=VMEM_LIMIT)


def _const_spec(a):
    nd = a.ndim
    return pl.BlockSpec(a.shape, lambda *_: (0,) * nd, pipeline_mode=pl.Buffered(1))


def _sigmoid(v):
    return 1.0 / (1.0 + jnp.exp(-v))


def _modulate(xv, shift, scale):
    ms = jnp.mean(xv * xv, axis=-1, keepdims=True)
    return xv * lax.rsqrt(ms + EPS) * (1.0 + scale) + shift


def _dot(a, b):
    return jnp.dot(a, b, preferred_element_type=F32)


def _group_meansq(z, g):
    z2 = z * z
    hi = z2.astype(BF16)
    lo = (z2 - hi.astype(F32)).astype(BF16)
    return _dot(hi, g) + _dot(lo, g)


def _group_rsqrt(z, g):
    w = g.shape[0]
    parts = [lax.rsqrt(_group_meansq(z[:, c:c + w], g) + EPS) for c in range(0, z.shape[1], w)]
    return jnp.concatenate(parts, axis=1) if len(parts) > 1 else parts[0]


def _ada_kernel(c_ref, w_ref, b_ref, o_ref):
    cv = c_ref[...]
    s = cv * _sigmoid(cv)
    o_ref[0] = _dot(s, w_ref[0]) + b_ref[0]


def _ada_call(cs, ada_w, ada_b):
    depth, d, n = ada_w.shape
    tn = n // 4
    return pl.pallas_call(
        _ada_kernel,
        out_shape=jax.ShapeDtypeStruct((depth, cs.shape[0], n), F32),
        grid=(depth, n // tn),
        in_specs=[pl.BlockSpec(cs.shape, lambda l, j: (0, 0)),
                  pl.BlockSpec((1, d, tn), lambda l, j: (l, 0, j)),
                  pl.BlockSpec((1, 1, tn), lambda l, j: (l, 0, j))],
        out_specs=pl.BlockSpec((1, cs.shape[0], tn), lambda l, j: (l, 0, j)),
        compiler_params=_cparams(2),
        name="ada_mod",
    )(cs, ada_w, ada_b.reshape(depth, 1, n))


def _pre_ret_kernel(x_ref, mod_ref, cs_ref, w_ref, wp_ref, q_ref, k_ref, v_ref, g_ref):
    mod = mod_ref[0]
    h = _modulate(x_ref[0], mod[0:1], mod[1:2]).astype(BF16)
    dk = q_ref.shape[2] // RET_HEADS
    nqk = RET_HEADS * dk
    nv = v_ref.shape[2]
    cos = cs_ref[:, 0:dk]
    sin = cs_ref[:, dk:2 * dk]
    zq = _dot(h, w_ref[:, 0:nqk])
    zk = _dot(h, w_ref[:, nqk:2 * nqk])
    o2 = 2 * nqk + 2 * nv
    zqp = _dot(h, wp_ref[:, 0:nqk])
    zkp = _dot(h, wp_ref[:, nqk:2 * nqk])
    kscale = dk ** -0.5
    for hh in range(RET_HEADS):
        sl = slice(hh * dk, (hh + 1) * dk)
        q_ref[0, :, sl] = (zq[:, sl] * cos + zqp[:, sl] * sin).astype(BF16)
        k_ref[0, :, sl] = ((zk[:, sl] * cos + zkp[:, sl] * sin) * kscale).astype(BF16)
    v_ref[0] = _dot(h, w_ref[:, 2 * nqk:2 * nqk + nv]).astype(BF16)
    g = _dot(h, w_ref[:, 2 * nqk + nv:o2])
    g_ref[0] = (g * _sigmoid(g)).astype(BF16)


def _pre_diff_kernel(x_ref, mod_ref, cs_ref, w_ref, wp_ref, gq_ref, gk_ref, gm_ref,
                     qa_ref, qb_ref, k_ref, v_ref, *, qscale):
    mod = mod_ref[0]
    h = _modulate(x_ref[0], mod[0:1], mod[1:2]).astype(BF16)
    n = k_ref.shape[2]
    tm = x_ref.shape[1]
    cos = cs_ref[:, 0:LANES]
    sin = cs_ref[:, LANES:2 * LANES]
    gq = gq_ref[...]
    gk = gk_ref[...]
    qc, qs = gq[0:1] * cos * qscale, gq[1:2] * sin * qscale
    kc, ks = gk[0:1] * cos, gk[1:2] * sin
    gm = gm_ref[...]
    first = lax.broadcasted_iota(jnp.int32, (tm, LANES), 1) < LANES // 2
    zq = _dot(h, w_ref[:, 0:n])
    zk = _dot(h, w_ref[:, n:2 * n])
    zqp = _dot(h, wp_ref[:, 0:n])
    zkp = _dot(h, wp_ref[:, n:2 * n])
    rq = _group_rsqrt(zq, gm)
    rk = _group_rsqrt(zk, gm)
    for hh in range(n // LANES):
        sl = slice(hh * LANES, (hh + 1) * LANES)
        q = rq[:, sl] * (zq[:, sl] * qc + zqp[:, sl] * qs)
        qa_ref[0, :, sl] = jnp.where(first, q, 0.0).astype(BF16)
        qb_ref[0, :, sl] = jnp.where(first, 0.0, q).astype(BF16)
        k = rk[:, sl] * (zk[:, sl] * kc + zkp[:, sl] * ks)
        k_ref[0, :, sl] = k.astype(BF16)
    v_ref[0] = _dot(h, w_ref[:, 2 * n:3 * n]).astype(BF16)


def _pre_na_kernel(x_ref, mod_ref, w_ref, gq_ref, gk_ref, gm_ref,
                   q_ref, k_ref, v_ref, *, qscale):
    mod = mod_ref[0]
    h = _modulate(x_ref[0], mod[0:1], mod[1:2]).astype(BF16)
    n = k_ref.shape[2]
    gq = gq_ref[...] * qscale
    gk = gk_ref[...]
    gm = gm_ref[...]
    zq = _dot(h, w_ref[:, 0:n])
    zk = _dot(h, w_ref[:, n:2 * n])
    rq = _group_rsqrt(zq, gm)
    rk = _group_rsqrt(zk, gm)
    for hh in range(n // LANES):
        sl = slice(hh * LANES, (hh + 1) * LANES)
        q_ref[0, :, sl] = (zq[:, sl] * rq[:, sl] * gq).astype(BF16)
        k_ref[0, :, sl] = (zk[:, sl] * rk[:, sl] * gk).astype(BF16)
    v_ref[0] = _dot(h, w_ref[:, 2 * n:3 * n]).astype(BF16)


def _pre_mla_kernel(x_ref, mod_ref, cs_ref, wd_ref, gqr_ref, gkv_ref, wuq_ref, wukv_ref,
                    gq_ref, gk_ref, gmq_ref, gmk_ref, q_ref, k_ref, v_ref, *, qscale):
    mod = mod_ref[0]
    h = _modulate(x_ref[0], mod[0:1], mod[1:2]).astype(BF16)
    tm = x_ref.shape[1]
    n = q_ref.shape[2]
    z = _dot(h, wd_ref[...])
    o1 = MLA_Q_RANK
    o2 = o1 + MLA_KV_RANK
    zq = z[:, 0:o1]
    qn = (zq * lax.rsqrt(jnp.mean(zq * zq, axis=-1, keepdims=True) + EPS) * gqr_ref[...]).astype(BF16)
    zc = z[:, o1:o2]
    cn = (zc * lax.rsqrt(jnp.mean(zc * zc, axis=-1, keepdims=True) + EPS) * gkv_ref[...]).astype(BF16)
    zr = z[:, o2:o2 + LANES]
    zrp = z[:, o2 + LANES:o2 + 2 * LANES]
    cos = cs_ref[:, 0:LANES]
    sin = cs_ref[:, LANES:2 * LANES]
    gq = gq_ref[...]
    gk = gk_ref[...]
    qc, qs = gq[0:1] * cos * qscale, gq[1:2] * sin * qscale
    msr = jnp.sum(zr * zr, axis=-1, keepdims=True) * (1.0 / MLA_ROPE)
    krope = lax.rsqrt(msr + EPS) * (zr * (gk[0:1] * cos) + zrp * (gk[1:2] * sin))
    gmq = gmq_ref[...]
    gmk = gmk_ref[...]
    one_lane = (lax.broadcasted_iota(jnp.int32, (tm, LANES), 1) == MLA_V).astype(F32)
    uq = _dot(qn, wuq_ref[:, 0:n])
    uqp = _dot(qn, wuq_ref[:, n:2 * n])
    uk = _dot(cn, wukv_ref[:, 0:n])
    uv = _dot(cn, wukv_ref[:, n:2 * n])
    rq = _group_rsqrt(uq, gmq)
    rk = _group_rsqrt(uk, gmk)
    for hh in range(n // LANES):
        sl = slice(hh * LANES, (hh + 1) * LANES)
        q = rq[:, sl] * (uq[:, sl] * qc + uqp[:, sl] * qs)
        q_ref[0, :, sl] = q.astype(BF16)
        k = uk[:, sl] * rk[:, sl] * gk[0:1] + krope
        k_ref[0, :, sl] = k.astype(BF16)
        v_ref[0, :, sl] = (uv[:, sl] + one_lane).astype(BF16)


def _pre_call(kern, x, mod, tables, consts, out_widths, name):
    nb, n, d = x.shape
    tm = min(TM, n)
    assert n % tm == 0
    tok = lambda w: pl.BlockSpec((1, tm, w), lambda b, i: (b, i, 0))
    in_specs = [tok(d), pl.BlockSpec((1, ADA_CHUNKS, d), lambda b, i: (b, 0, 0))]
    in_specs += [pl.BlockSpec((tm, t.shape[1]), lambda b, i: (i, 0)) for t in tables]
    in_specs += [_const_spec(a) for a in consts]
    return pl.pallas_call(
        kern,
        out_shape=[jax.ShapeDtypeStruct((nb, n, w), BF16) for w in out_widths],
        grid=(nb, n // tm),
        in_specs=in_specs,
        out_specs=[tok(w) for w in out_widths],
        compiler_params=_cparams(2),
        name=name,
    )(x, mod, *tables, *consts)


def _scalar_vec(s):
    return jnp.full((1, 1), s, F32)


def _ret_lat_kernel(lg_ref, qf_ref, kf_ref, vf_ref, qb_ref, kb_ref, vb_ref, kc_ref, vc_ref,
                    of_ref, ob_ref, sf_s, sb_s):
    c = pl.program_id(1)
    cn = qf_ref.shape[1]
    n_ctx = kc_ref.shape[1]
    dk = qf_ref.shape[2] // RET_HEADS
    dv = vf_ref.shape[2] // RET_HEADS

    @pl.when(c == 0)
    def _():
        pos = lax.broadcasted_iota(jnp.int32, (n_ctx, 1), 0).astype(F32)
        for hh in range(RET_HEADS):
            lf = _scalar_vec(lg_ref[0, hh])
            lb = _scalar_vec(lg_ref[1, hh])
            kc = kc_ref[0, :, hh * dk:(hh + 1) * dk].astype(F32)
            vc = vc_ref[0, :, hh * dv:(hh + 1) * dv]
            wf = jnp.exp((n_ctx - 1.0 - pos) * lf)
            wb = jnp.exp(pos * lb)
            sf_s[hh] = lax.dot_general((kc * wf).astype(BF16), vc, _TN, preferred_element_type=F32)
            sb_s[hh] = lax.dot_general((kc * wb).astype(BF16), vc, _TN, preferred_element_type=F32)

    ii = lax.broadcasted_iota(jnp.int32, (cn, cn), 0)
    jj = lax.broadcasted_iota(jnp.int32, (cn, cn), 1)
    rel = (ii - jj).astype(F32)
    idx = lax.broadcasted_iota(jnp.int32, (cn, 1), 0).astype(F32)

    for hh in range(RET_HEADS):
        lf = _scalar_vec(lg_ref[0, hh])
        lb = _scalar_vec(lg_ref[1, hh])
        ksl = slice(hh * dk, (hh + 1) * dk)
        vsl = slice(hh * dv, (hh + 1) * dv)

        q, k, v = qf_ref[0, :, ksl], kf_ref[0, :, ksl], vf_ref[0, :, vsl]
        dec = jnp.where(rel >= 0, jnp.exp(jnp.maximum(rel, 0.0) * lf), 0.0)
        att = lax.dot_general(q, k, _NT, preferred_element_type=F32) * dec
        s = sf_s[hh]
        of_ref[0, :, vsl] = _dot(att.astype(BF16), v) + _dot(q, s.astype(BF16)) * jnp.exp((idx + 1.0) * lf)
        kd = (k.astype(F32) * jnp.exp((cn - 1.0 - idx) * lf)).astype(BF16)
        sf_s[hh] = s * jnp.exp(cn * lf) + lax.dot_general(kd, v, _TN, preferred_element_type=F32)

        q, k, v = qb_ref[0, :, ksl], kb_ref[0, :, ksl], vb_ref[0, :, vsl]
        dec = jnp.where(rel <= 0, jnp.exp(jnp.maximum(-rel, 0.0) * lb), 0.0)
        att = lax.dot_general(q, k, _NT, preferred_element_type=F32) * dec
        s = sb_s[hh]
        ob_ref[0, :, vsl] = _dot(att.astype(BF16), v) + _dot(q, s.astype(BF16)) * jnp.exp((cn - idx) * lb)
        kd = (k.astype(F32) * jnp.exp(idx * lb)).astype(BF16)
        sb_s[hh] = s * jnp.exp(cn * lb) + lax.dot_general(kd, v, _TN, preferred_element_type=F32)


def _ret_lat_call(lg, q, k, v, kc, vc):
    b, n, nqk = q.shape
    dk = nqk // RET_HEADS
    nv = v.shape[2]
    dv = nv // RET_HEADS
    n_ctx = kc.shape[1]
    cn = RET_CHUNK
    nc = n // cn
    fwd = lambda w: pl.BlockSpec((1, cn, w), lambda bb, c: (bb, c, 0))
    bwd = lambda w: pl.BlockSpec((1, cn, w), lambda bb, c: (bb, nc - 1 - c, 0))
    ctx = lambda w: pl.BlockSpec((1, n_ctx, w), lambda bb, c: (bb, 0, 0))
    return pl.pallas_call(
        _ret_lat_kernel,
        out_shape=[jax.ShapeDtypeStruct((b, n, nv), F32)] * 2,
        grid=(b, nc),
        in_specs=[pl.BlockSpec(memory_space=pltpu.SMEM),
                  fwd(nqk), fwd(nqk), fwd(nv), bwd(nqk), bwd(nqk), bwd(nv), ctx(nqk), ctx(nv)],
        out_specs=[fwd(nv), bwd(nv)],
        scratch_shapes=[pltpu.VMEM((RET_HEADS, dk, dv), F32), pltpu.VMEM((RET_HEADS, dk, dv), F32)],
        compiler_params=_cparams(2),
        name="ret_lat",
    )(lg, q, k, v, q, k, v, kc, vc)


def _ret_ctx_kernel(lg_ref, q_ref, k_ref, v_ref, o_ref):
    hh = pl.program_id(1)
    lf = _scalar_vec(lg_ref[0, hh])
    lb = _scalar_vec(lg_ref[1, hh])
    n = q_ref.shape[1]
    ii = lax.broadcasted_iota(jnp.int32, (n, n), 0)
    jj = lax.broadcasted_iota(jnp.int32, (n, n), 1)
    rel = (ii - jj).astype(F32)
    dec = (jnp.where(rel >= 0, jnp.exp(jnp.maximum(rel, 0.0) * lf), 0.0)
           + jnp.where(rel <= 0, jnp.exp(jnp.maximum(-rel, 0.0) * lb), 0.0))
    att = lax.dot_general(q_ref[0], k_ref[0], _NT, preferred_element_type=F32) * dec
    o_ref[0] = _dot(att.astype(BF16), v_ref[0])


def _ret_ctx_call(lg, q, k, v):
    b, n, nqk = q.shape
    dk = nqk // RET_HEADS
    dv = v.shape[2] // RET_HEADS
    spec = lambda w: pl.BlockSpec((1, n, w), lambda bb, hh: (bb, 0, hh))
    return pl.pallas_call(
        _ret_ctx_kernel,
        out_shape=jax.ShapeDtypeStruct((b, n, RET_HEADS * dv), F32),
        grid=(b, RET_HEADS),
        in_specs=[pl.BlockSpec(memory_space=pltpu.SMEM), spec(dk), spec(dk), spec(dv)],
        out_specs=spec(dv),
        compiler_params=_cparams(2),
        name="ret_ctx",
    )(lg, q, k, v)


def _lane_tiles(v, n):
    return jnp.concatenate([v] * n, axis=1) if n > 1 else v


def _softmax_update(s, m_s, l_s):
    n = s.shape[1] // LANES
    m_prev = m_s[...]
    m_new = jnp.maximum(m_prev, jnp.max(s, axis=-1, keepdims=True))
    alpha = jnp.exp2(m_prev - m_new)
    m_s[...] = m_new
    p = jnp.exp2(s - _lane_tiles(m_new, n))
    if l_s is not None:
        part = p[:, 0:LANES]
        for t in range(1, n):
            part = part + p[:, t * LANES:(t + 1) * LANES]
        l_s[...] = alpha * l_s[...] + part
    return alpha, p


def _flash_streams(streams, kl_ref, vl_ref, kc_ref, vc_ref, n_main):
    for q, m_s, l_s, acc_s, _, _, _ in streams:
        m_s[...] = jnp.full(m_s.shape, NEG, F32)
        acc_s[...] = jnp.zeros(acc_s.shape, F32)
        if l_s is not None:
            l_s[...] = jnp.zeros(l_s.shape, F32)
        s = lax.dot_general(q, kc_ref[0], _NT, preferred_element_type=F32)
        alpha, p = _softmax_update(s, m_s, l_s)
        acc_s[...] = alpha * acc_s[...] + _dot(p.astype(BF16), vc_ref[0])
    if not n_main:
        return
    assert n_main >= 2 and n_main % 2 == 0

    def chunk(ref, j):
        off = j * TK if isinstance(j, int) else pl.multiple_of(j * TK, TK)
        return ref[0, pl.ds(off, TK), :]

    def scores(j, slot):
        k_c = chunk(kl_ref, j)
        for q, _, _, _, s_scr, _, _ in streams:
            s_scr[slot] = lax.dot_general(q, k_c, _NT, preferred_element_type=F32)

    def softmax(slot):
        for _, m_s, l_s, _, s_scr, p_scr, al_scr in streams:
            alpha, p = _softmax_update(s_scr[slot], m_s, l_s)
            al_scr[slot] = alpha
            p_scr[slot] = p.astype(BF16)

    def values(j, slot):
        v_c = chunk(vl_ref, j)
        for _, _, _, acc_s, _, p_scr, al_scr in streams:
            acc_s[...] = al_scr[slot] * acc_s[...] + _dot(p_scr[slot], v_c)

    scores(0, 0)
    scores(1, 1)
    softmax(0)

    def step(j, par):
        scores(j + 2, par)
        softmax(1 - par)
        values(j, par)

    n_pipe = n_main - 2
    n_loop = n_pipe // FLASH_UNROLL

    def body(t, carry):
        for u in range(FLASH_UNROLL):
            step(t * FLASH_UNROLL + u, u % 2)
        return carry

    if n_loop:
        lax.fori_loop(0, n_loop, body, 0)
    for j in range(n_loop * FLASH_UNROLL, n_pipe):
        step(j, j % 2)
    softmax(1)
    values(n_main - 2, 0)
    values(n_main - 1, 1)


def _stream_scratch(tq, n_main, with_l):
    stat = [pltpu.VMEM((tq, LANES), F32)] * (3 if with_l else 2)
    if not n_main:
        return stat
    return stat + [pltpu.VMEM((2, tq, TK), F32), pltpu.VMEM((2, tq, TK), BF16), pltpu.VMEM((2, tq, LANES), F32)]


def _mla_flash_kernel(*refs, n_main):
    if n_main:
        q_ref, kl_ref, vl_ref, kc_ref, vc_ref, o_ref, m_s, acc_s, s_scr, p_scr, al_scr = refs
    else:
        q_ref, kc_ref, vc_ref, o_ref, m_s, acc_s = refs
        kl_ref = vl_ref = s_scr = p_scr = al_scr = None
    _flash_streams([(q_ref[0], m_s, None, acc_s, s_scr, p_scr, al_scr)], kl_ref, vl_ref, kc_ref, vc_ref, n_main)
    acc = acc_s[...]
    o_ref[0] = (acc / acc[:, MLA_V:MLA_V + 1]).astype(BF16)


def _mla_flash_call(q, k_ctx, v_ctx, k_lat=None, v_lat=None):
    b, nq, w = q.shape
    nh = w // LANES
    n_ctx = k_ctx.shape[1]
    tq = min(TQ, nq)
    assert nq % tq == 0
    qspec = pl.BlockSpec((1, tq, LANES), lambda bb, hh, i: (bb, i, hh))
    full = lambda n: pl.BlockSpec((1, n, LANES), lambda bb, hh, i: (bb, 0, hh))
    args, in_specs, n_main = [q], [qspec], 0
    if k_lat is not None:
        n_lat = k_lat.shape[1]
        assert n_lat % TK == 0
        n_main = n_lat // TK
        args += [k_lat, v_lat]
        in_specs += [full(n_lat), full(n_lat)]
    args += [k_ctx, v_ctx]
    in_specs += [full(n_ctx), full(n_ctx)]
    return pl.pallas_call(
        functools.partial(_mla_flash_kernel, n_main=n_main),
        out_shape=jax.ShapeDtypeStruct((b, nq, w), BF16),
        grid=(b, nh, nq // tq),
        in_specs=in_specs,
        out_specs=qspec,
        scratch_shapes=_stream_scratch(tq, n_main, with_l=False),
        compiler_params=_cparams(3),
        name="mla_attn" if n_main else "mla_attn_ctx",
    )(*args)


def _diff_flash_kernel(*refs, n_main, out_scale):
    if n_main:
        (lam_ref, qa_ref, qb_ref, kl_ref, vl_ref, kc_ref, vc_ref, g_ref, o_ref,
         m1, l1, a1, m2, l2, a2, s_scr, p_scr, al_scr) = refs
    else:
        (lam_ref, qa_ref, qb_ref, kc_ref, vc_ref, g_ref, o_ref, m1, l1, a1, m2, l2, a2) = refs
        kl_ref = vl_ref = s_scr = p_scr = al_scr = None
    for q_ref, m_s, l_s, a_s in ((qa_ref, m1, l1, a1), (qb_ref, m2, l2, a2)):
        _flash_streams([(q_ref[0], m_s, l_s, a_s, s_scr, p_scr, al_scr)], kl_ref, vl_ref, kc_ref, vc_ref, n_main)
    rowsum = lambda l_s: jnp.sum(l_s[...], axis=-1, keepdims=True)
    o = a1[...] / rowsum(l1) - lam_ref[0] * (a2[...] / rowsum(l2))
    ms = jnp.mean(o * o, axis=-1, keepdims=True)
    o_ref[0] = (o * lax.rsqrt(ms + EPS) * (g_ref[...] * out_scale)).astype(BF16)


def _diff_flash_call(lam, subln_g, out_scale, qa, qb, k_ctx, v_ctx, k_lat=None, v_lat=None):
    b, nq, w = qa.shape
    nh = w // LANES
    n_ctx = k_ctx.shape[1]
    tq = min(TQ, nq)
    assert nq % tq == 0
    qspec = pl.BlockSpec((1, tq, LANES), lambda bb, hh, i: (bb, i, hh))
    full = lambda n: pl.BlockSpec((1, n, LANES), lambda bb, hh, i: (bb, 0, hh))
    args = [lam, qa, qb]
    in_specs = [pl.BlockSpec(memory_space=pltpu.SMEM), qspec, qspec]
    n_main = 0
    if k_lat is not None:
        n_lat = k_lat.shape[1]
        assert n_lat % TK == 0
        n_main = n_lat // TK
        args += [k_lat, v_lat]
        in_specs += [full(n_lat), full(n_lat)]
    args += [k_ctx, v_ctx, subln_g]
    in_specs += [full(n_ctx), full(n_ctx), pl.BlockSpec((1, LANES), lambda bb, hh, i: (0, 0))]
    stat = _stream_scratch(tq, 0, with_l=True)
    stage = _stream_scratch(tq, n_main, with_l=True)[len(stat):]
    return pl.pallas_call(
        functools.partial(_diff_flash_kernel, n_main=n_main, out_scale=out_scale),
        out_shape=jax.ShapeDtypeStruct((b, nq, w), BF16),
        grid=(b, nh, nq // tq),
        in_specs=in_specs,
        out_specs=qspec,
        scratch_shapes=stat + stat + stage,
        compiler_params=_cparams(3),
        name="diff_attn" if n_main else "diff_attn_ctx",
    )(*args)


def _na_kernel(*refs, n_loc):
    q_ref = refs[0]
    k_loc = refs[1:1 + n_loc]
    v_loc = refs[1 + n_loc:1 + 2 * n_loc]
    rest = refs[1 + 2 * n_loc:]
    if n_loc:
        tab_ref, kc_ref, vc_ref, o_ref = rest
    else:
        kc_ref, vc_ref, o_ref = rest
    q = q_ref[0]
    tq = q.shape[0]
    first = lax.broadcasted_iota(jnp.int32, (tq, LANES), 1) < LANES // 2
    zero = jnp.zeros_like(q)
    outs = []
    for hh in range(2):
        qh = jnp.where(first, q, zero) if hh == 0 else jnp.where(first, zero, q)
        ss = []
        for j in range(n_loc):
            tkb = k_loc[j].shape[1]
            s = lax.dot_general(qh, k_loc[j][0], _NT, preferred_element_type=F32)
            ss.append(s + tab_ref[0, hh, :, j * tkb:(j + 1) * tkb])
        ss.append(lax.dot_general(qh, kc_ref[0], _NT, preferred_element_type=F32))
        m = functools.reduce(jnp.maximum, [jnp.max(s, axis=-1, keepdims=True) for s in ss])
        ps = [jnp.exp2(s - m) for s in ss]
        l = functools.reduce(lambda a, c: a + c, [jnp.sum(p, axis=-1, keepdims=True) for p in ps])
        vs = [r[0] for r in v_loc] + [vc_ref[0]]
        o = functools.reduce(lambda a, c: a + c, [_dot(p.astype(BF16), v) for p, v in zip(ps, vs)])
        outs.append(o / l)
    o_ref[0] = jnp.where(first, outs[0], outs[1]).astype(BF16)


def _na_call(q, k_ctx, v_ctx, k_lat=None, v_lat=None, table=None):
    b, nq, w = q.shape
    ng = w // LANES
    n_ctx = k_ctx.shape[1]
    if k_lat is None:
        tq, nblk, n_loc = nq, 1, 0
    else:
        tq = NA_ROWS * GRID_W
        nblk = nq // tq
        n_loc = 3
        assert nq % tq == 0 and nblk >= 3
    qspec = pl.BlockSpec((1, tq, LANES), lambda g, bb, i: (bb, i, g))
    prev = pl.BlockSpec((1, tq, LANES), lambda g, bb, i: (bb, jnp.maximum(i - 1, 0), g))
    nxt = pl.BlockSpec((1, tq, LANES), lambda g, bb, i: (bb, jnp.minimum(i + 1, nblk - 1), g))
    cspec = pl.BlockSpec((1, n_ctx, LANES), lambda g, bb, i: (bb, 0, g))
    args, in_specs = [q], [qspec]
    if n_loc:
        args += [k_lat] * 3 + [v_lat] * 3 + [table]
        variant = lambda i: jnp.where(i == 0, 0, jnp.where(i == nblk - 1, 2, 1))
        in_specs += [prev, qspec, nxt] * 2
        in_specs += [pl.BlockSpec((1, 2, tq, 3 * tq), lambda g, bb, i: (variant(i), g, 0, 0))]
    args += [k_ctx, v_ctx]
    in_specs += [cspec, cspec]
    return pl.pallas_call(
        functools.partial(_na_kernel, n_loc=n_loc),
        out_shape=jax.ShapeDtypeStruct((b, nq, w), BF16),
        grid=(ng, b, nblk),
        in_specs=in_specs,
        out_specs=qspec,
        compiler_params=_cparams(3),
        name="na_attn" if n_loc else "na_attn_ctx",
    )(*args)


def _na_table(rel_bias, rows):
    r_blk = NA_ROWS
    nblk = rows // r_blk
    wh = min(NA_WIN_H, rows)
    col = np.arange(GRID_W)
    c0 = np.clip(col - NA_WIN_W // 2, 0, GRID_W - NA_WIN_W)
    dcol = col[None, :] - col[:, None]
    col_ok = (col[None, :] >= c0[:, None]) & (col[None, :] < c0[:, None] + NA_WIN_W)
    dc_idx = np.clip(dcol + NA_WIN_W - 1, 0, 2 * NA_WIN_W - 2)
    nh = rel_bias.shape[0]
    t1 = jnp.where(col_ok[None, None], rel_bias[:, :, dc_idx] * LOG2E, NEG)
    t1 = jnp.concatenate([t1, jnp.full((nh, 1, GRID_W, GRID_W), NEG, F32)], axis=1)
    masked = 2 * NA_WIN_H - 1
    idx = np.full((3, r_blk, 3 * r_blk), masked, np.int32)
    for v, i in enumerate((0, 1, nblk - 1)):
        for qr in range(r_blk):
            r = i * r_blk + qr
            r0 = min(max(r - NA_WIN_H // 2, 0), rows - wh)
            for slot, blk in enumerate((i - 1, i, i + 1)):
                if blk < 0 or blk >= nblk:
                    continue
                for kr_l in range(r_blk):
                    kr = blk * r_blk + kr_l
                    if r0 <= kr < r0 + wh:
                        idx[v, qr, slot * r_blk + kr_l] = kr - r + NA_WIN_H - 1
    tab = t1[:, idx]
    tab = jnp.transpose(tab, (1, 0, 2, 4, 3, 5))
    return tab.reshape(3, nh, r_blk * GRID_W, 3 * r_blk * GRID_W)


def _post_kernel(*refs, n_o, ret):
    o_refs = refs[:n_o]
    refs = refs[n_o:]
    if ret:
        gs_ref, ng_ref = refs[:2]
        refs = refs[2:]
    x_ref, mod_ref, wo_ref, w1_ref, w2_ref, out_ref = refs
    mod = mod_ref[0]
    if ret:
        o = functools.reduce(lambda a, c: a + c, [r[0] for r in o_refs])
        dv = o.shape[1] // RET_HEADS
        y = None
        for hh in range(RET_HEADS):
            sl = slice(hh * dv, (hh + 1) * dv)
            oh = o[:, sl]
            d = oh - jnp.mean(oh, axis=-1, keepdims=True)
            var = jnp.mean(d * d, axis=-1, keepdims=True)
            on = d * lax.rsqrt(var + EPS) * ng_ref[:, sl] * gs_ref[0, :, sl].astype(F32)
            part = _dot(on.astype(BF16), wo_ref[sl, :])
            y = part if y is None else y + part
    else:
        y = _dot(o_refs[0][0], wo_ref[...])
    x1 = x_ref[0] + mod[2:3] * y
    h2 = _modulate(x1, mod[3:4], mod[4:5]).astype(BF16)
    fh = w2_ref.shape[0]
    ua = _dot(h2, w1_ref[:, 0:fh])
    ug = _dot(h2, w1_ref[:, fh:2 * fh])
    act = (ua * _sigmoid(ua) * ug).astype(BF16)
    out_ref[0] = x1 + mod[5:6] * _dot(act, w2_ref[...])


def _post_call(os_, x, mod, wo, w1, w2, gs=None, ng=None, name="post"):
    nb, n, d = x.shape
    tm = min(TM, n)
    assert n % tm == 0
    tok = lambda w: pl.BlockSpec((1, tm, w), lambda b, i: (b, i, 0))
    ret = gs is not None
    args = list(os_)
    in_specs = [tok(o.shape[2]) for o in os_]
    if ret:
        args += [gs, ng]
        in_specs += [tok(gs.shape[2]), _const_spec(ng)]
    args += [x, mod, wo, w1, w2]
    in_specs += [tok(d), pl.BlockSpec((1, ADA_CHUNKS, d), lambda b, i: (b, 0, 0)),
                 _const_spec(wo), _const_spec(w1), _const_spec(w2)]
    return pl.pallas_call(
        functools.partial(_post_kernel, n_o=len(os_), ret=ret),
        out_shape=jax.ShapeDtypeStruct((nb, n, d), F32),
        grid=(nb, n // tm),
        in_specs=in_specs,
        out_specs=tok(d),
        compiler_params=_cparams(2),
        name=name,
    )(*args)


def _rope_cos_sin(n, d):
    quarter = d // 4
    rows = n // GRID_W
    inv_freq = jnp.power(ROPE_BASE, -jnp.arange(quarter, dtype=F32) / quarter)
    ar = jnp.arange(rows, dtype=F32)[:, None] * inv_freq[None, :]
    ac = jnp.arange(GRID_W, dtype=F32)[:, None] * inv_freq[None, :]
    by_row = lambda t: jnp.broadcast_to(t[:, None, :], (rows, GRID_W, quarter)).reshape(n, quarter)
    by_col = lambda t: jnp.broadcast_to(t[None, :, :], (rows, GRID_W, quarter)).reshape(n, quarter)
    cr, sr, cc, sc = by_row(jnp.cos(ar)), by_row(jnp.sin(ar)), by_col(jnp.cos(ac)), by_col(jnp.sin(ac))
    cos = jnp.concatenate([cr, cr, cc, cc], axis=-1)
    sin = jnp.concatenate([-sr, sr, -sc, sc], axis=-1)
    return cos, sin


def _rope_partner(d):
    half, quarter = d // 2, d // 4
    l = np.arange(d)
    return np.where((l % half) < quarter, l + quarter, l - quarter)


def _tiled_partner(d, n):
    return (np.arange(n) // d) * d + _rope_partner(d)[np.arange(n) % d]


V7X_MXU_WIDTH = 256


def _group_matrix(blocks):
    g = np.zeros((V7X_MXU_WIDTH, V7X_MXU_WIDTH), np.float32)
    for base in range(0, V7X_MXU_WIDTH, LANES):
        for a, b in blocks:
            g[base + a:base + b, base + a:base + b] = 1.0 / (b - a)
    return jnp.asarray(g, BF16)


def _identity_tables(n, w):
    return jnp.concatenate([jnp.ones((n, w), F32), jnp.zeros((n, w), F32)], axis=-1)


def kernel(x, c, ctx, c_ctx, ada_w, ada_b, ret_w_in, ret_decay_logit, ret_norm_g, ret_w_out, diff_w_in, diff_q_norm_g, diff_k_norm_g, diff_lambda, diff_subln_g, diff_w_out, na_w_in, na_q_norm_g, na_k_norm_g, na_rel_bias, na_w_out, mla_w_down, mla_q_norm_g, mla_kv_norm_g, mla_w_uq, mla_w_ukv, mla_qk_norm_q, mla_qk_norm_k, mla_w_out, ffn_w_in, ffn_w_out):
    bsz, seq, d = x.shape
    n_ctx = ctx.shape[1]
    depth = ada_w.shape[0]
    assert seq % GRID_W == 0 and bsz + 1 <= 8

    cs = jnp.zeros((8, d), F32).at[:bsz].set(c).at[bsz].set(c_ctx)
    mods = _ada_call(cs, ada_w, ada_b)

    x_lat = x
    x_ctx = ctx.reshape(1, bsz * n_ctx, d)
    as_ctx = lambda a: a.reshape(bsz, n_ctx, a.shape[-1])
    g64 = _group_matrix([(0, 64), (64, 128)])

    for i in range(depth):
        need_ctx = i < depth - 1
        mod_l = mods[i, :bsz].reshape(bsz, ADA_CHUNKS, d)
        mod_c = mods[i, bsz:bsz + 1].reshape(1, ADA_CHUNKS, d)
        kind, j = i % N_MIXERS, i // N_MIXERS
        gs_l = gs_c = ng = None

        if kind == 0:
            w = ret_w_in[j]
            dk = d // RET_HEADS
            nqk = RET_HEADS * dk
            perm = _tiled_partner(dk, nqk)
            wcat = [w.astype(BF16), w[:, np.concatenate([perm, nqk + perm])].astype(BF16)]
            cos, sin = _rope_cos_sin(seq, dk)
            tab_l = jnp.concatenate([cos, sin], axis=-1)
            tab_c = _identity_tables(bsz * n_ctx, dk)
            widths = [nqk, nqk, 2 * nqk, 2 * nqk]
            ql, kl, vl, gs_l = _pre_call(_pre_ret_kernel, x_lat, mod_l, [tab_l], wcat, widths, "pre_ret")
            qc, kc, vc, gs_c = _pre_call(_pre_ret_kernel, x_ctx, mod_c, [tab_c], wcat, widths, "pre_ret_ctx")
            qc, kc, vc = as_ctx(qc), as_ctx(kc), as_ctx(vc)
            lg = jax.nn.log_sigmoid(ret_decay_logit[j].astype(F32))
            o_l = list(_ret_lat_call(lg, ql, kl, vl, kc, vc))
            o_c = [_ret_ctx_call(lg, qc, kc, vc).reshape(1, bsz * n_ctx, -1)] if need_ctx else None
            ng = ret_norm_g[j].reshape(1, -1)
            wo = ret_w_out[j].astype(BF16)
        elif kind == 1:
            w = diff_w_in[j]
            dh = d // (2 * DIFF_HEADS)
            nq = 2 * DIFF_HEADS * dh
            perm = _tiled_partner(dh, nq)
            wcat = [w.astype(BF16), w[:, np.concatenate([perm, nq + perm])].astype(BF16)]
            cos, sin = _rope_cos_sin(seq, dh)
            tab_l = jnp.concatenate([cos, cos, sin, sin], axis=-1)
            tab_c = _identity_tables(bsz * n_ctx, LANES)
            pl_ = _rope_partner(dh)
            gq = jnp.stack([jnp.tile(diff_q_norm_g[j], 2), jnp.tile(diff_q_norm_g[j][pl_], 2)])
            gk = jnp.stack([jnp.tile(diff_k_norm_g[j], 2), jnp.tile(diff_k_norm_g[j][pl_], 2)])
            kern = functools.partial(_pre_diff_kernel, qscale=dh ** -0.5 * LOG2E)
            widths = [nq, nq, nq, nq]
            qa, qb, kl, vl = _pre_call(kern, x_lat, mod_l, [tab_l], wcat + [gq, gk, g64], widths, "pre_diff")
            qac, qbc, kc, vc = _pre_call(kern, x_ctx, mod_c, [tab_c], wcat + [gq, gk, g64], widths, "pre_diff_ctx")
            qac, qbc, kc, vc = as_ctx(qac), as_ctx(qbc), as_ctx(kc), as_ctx(vc)
            lv = diff_lambda[j].astype(F32)
            lambda_init = 0.8 - 0.6 * math.exp(-0.3 * i)
            lam = (jnp.exp(jnp.sum(lv[0] * lv[1])) - jnp.exp(jnp.sum(lv[2] * lv[3])) + lambda_init).reshape(1)
            sg = diff_subln_g[j].reshape(1, -1)
            o_l = [_diff_flash_call(lam, sg, 1.0 - lambda_init, qa, qb, kc, vc, kl, vl)]
            o_c = ([_diff_flash_call(lam, sg, 1.0 - lambda_init, qac, qbc, kc, vc).reshape(1, bsz * n_ctx, -1)]
                   if need_ctx else None)
            wo = diff_w_out[j].astype(BF16)
        elif kind == 2:
            w = na_w_in[j].astype(BF16)
            dh = d // NA_HEADS
            gq = jnp.tile(na_q_norm_g[j], LANES // dh).reshape(1, LANES)
            gk = jnp.tile(na_k_norm_g[j], LANES // dh).reshape(1, LANES)
            kern = functools.partial(_pre_na_kernel, qscale=dh ** -0.5 * LOG2E)
            widths = [d, d, d]
            ql, kl, vl = _pre_call(kern, x_lat, mod_l, [], [w, gq, gk, g64], widths, "pre_na")
            qc, kc, vc = _pre_call(kern, x_ctx, mod_c, [], [w, gq, gk, g64], widths, "pre_na_ctx")
            qc, kc, vc = as_ctx(qc), as_ctx(kc), as_ctx(vc)
            table = _na_table(na_rel_bias[j].astype(F32), seq // GRID_W)
            o_l = [_na_call(ql, kc, vc, kl, vl, table)]
            o_c = [_na_call(qc, kc, vc).reshape(1, bsz * n_ctx, -1)] if need_ctx else None
            wo = na_w_out[j].astype(BF16)
        else:
            nh = MLA_HEADS
            qk = MLA_NOPE + MLA_ROPE
            wd = mla_w_down[j]
            o2 = MLA_Q_RANK + MLA_KV_RANK
            rope_cols = wd[:, o2:o2 + MLA_ROPE]
            pad = lambda a, lo, hi: jnp.pad(a, ((0, 0), (lo, hi)))
            rope_grp = pad(rope_cols, MLA_NOPE, LANES - qk)
            rope_grp_p = pad(rope_cols[:, _rope_partner(MLA_ROPE)], MLA_NOPE, LANES - qk)
            wdcat = jnp.concatenate([wd[:, :o2], rope_grp, rope_grp_p], axis=1).astype(BF16)

            def head_pad(a, width):
                r = a.shape[0]
                return jnp.pad(a.reshape(r, nh, width), ((0, 0), (0, 0), (0, LANES - width))).reshape(r, nh * LANES)

            wuq = mla_w_uq[j]
            perm = (np.arange(nh * qk) // qk) * qk + np.concatenate(
                [np.arange(MLA_NOPE), MLA_NOPE + _rope_partner(MLA_ROPE)])[np.arange(nh * qk) % qk]
            wuqcat = jnp.concatenate([head_pad(wuq, qk), head_pad(wuq[:, perm], qk)], axis=1).astype(BF16)
            wukv = mla_w_ukv[j].reshape(MLA_KV_RANK, nh, MLA_NOPE + MLA_V)
            wk = head_pad(wukv[:, :, :MLA_NOPE].reshape(MLA_KV_RANK, -1), MLA_NOPE)
            wv = head_pad(wukv[:, :, MLA_NOPE:].reshape(MLA_KV_RANK, -1), MLA_V)
            wukvcat = jnp.concatenate([wk, wv], axis=1).astype(BF16)

            def gain_rows(g):
                gp = jnp.concatenate([g[:MLA_NOPE], g[MLA_NOPE:][_rope_partner(MLA_ROPE)]])
                return jnp.stack([jnp.pad(g, (0, LANES - qk)), jnp.pad(gp, (0, LANES - qk))])

            cos, sin = _rope_cos_sin(seq, MLA_ROPE)
            ones_l = lambda n, w_: jnp.ones((n, w_), F32)
            zeros_l = lambda n, w_: jnp.zeros((n, w_), F32)
            tab_l = jnp.concatenate([ones_l(seq, MLA_NOPE), cos, ones_l(seq, LANES - qk),
                                     zeros_l(seq, MLA_NOPE), sin, zeros_l(seq, LANES - qk)], axis=-1)
            tab_c = _identity_tables(bsz * n_ctx, LANES)
            gmq = _group_matrix([(0, MLA_NOPE), (MLA_NOPE, qk)])
            gmk = _group_matrix([(0, MLA_NOPE)])
            consts = [wdcat, mla_q_norm_g[j].reshape(1, -1), mla_kv_norm_g[j].reshape(1, -1), wuqcat, wukvcat,
                      gain_rows(mla_qk_norm_q[j]), gain_rows(mla_qk_norm_k[j]), gmq, gmk]
            kern = functools.partial(_pre_mla_kernel, qscale=qk ** -0.5 * LOG2E)
            widths = [nh * LANES] * 3
            ql, kl, vl = _pre_call(kern, x_lat, mod_l, [tab_l], consts, widths, "pre_mla")
            qc, kc, vc = _pre_call(kern, x_ctx, mod_c, [tab_c], consts, widths, "pre_mla_ctx")
            qc, kc, vc = as_ctx(qc), as_ctx(kc), as_ctx(vc)
            o_l = [_mla_flash_call(ql, kc, vc, kl, vl)]
            o_c = [_mla_flash_call(qc, kc, vc).reshape(1, bsz * n_ctx, -1)] if need_ctx else None
            wo = jnp.pad(mla_w_out[j].reshape(nh, MLA_V, d), ((0, 0), (0, LANES - MLA_V), (0, 0)))
            wo = wo.reshape(nh * LANES, d).astype(BF16)

        w1 = ffn_w_in[i].astype(BF16)
        w2 = ffn_w_out[i].astype(BF16)
        x_lat = _post_call(o_l, x_lat, mod_l, wo, w1, w2, gs_l, ng, name="post")
        if need_ctx:
            x_ctx = _post_call(o_c, x_ctx, mod_c, wo, w1, w2, gs_c, ng, name="post_ctx")
    return x_lat
```

```python
import functools
import math

import numpy as np
import jax
import jax.numpy as jnp
from jax import lax
from jax.experimental import pallas as pl
from jax.experimental.pallas import tpu as pltpu

F32 = jnp.float32
BF16 = jnp.bfloat16

GRID_W = 64
ROPE_BASE = 10000.0
EPS = 1e-6
ADA_CHUNKS = 6
N_MIXERS = 4
RET_HEADS = 4
RET_CHUNK = 128
DIFF_HEADS = 8
NA_HEADS = 16
NA_WIN_H = 8
NA_WIN_W = 16
MLA_HEADS = 16
MLA_Q_RANK = 256
MLA_KV_RANK = 128
MLA_NOPE = 64
MLA_ROPE = 32
MLA_V = 64

LANES = 128
LOG2E = 1.4426950408889634
NEG = -1e30
V7X_VMEM_BYTES = 64 * 1024 * 1024
VMEM_LIMIT = V7X_VMEM_BYTES - 8 * 1024 * 1024

TM = 256
TQ = 512
TK = 512
NA_ROWS = 4
MLA_HEADS_PER_STEP = 2
FLASH_UNROLL = 10

_NT = (((1,), (1,)), ((), ()))
_TN = (((0,), (0,)), ((), ()))


def _cparams(n_axes):
    return pltpu.CompilerParams(dimension_semantics=("arbitrary",) * n_axes,
                                vmem_limit_bytes=VMEM_LIMIT)


def _const_spec(a):
    nd = a.ndim
    return pl.BlockSpec(a.shape, lambda *_: (0,) * nd, pipeline_mode=pl.Buffered(1))


def _sigmoid(v):
    return 1.0 / (1.0 + jnp.exp(-v))


def _modulate(xv, shift, scale):
    ms = jnp.mean(xv * xv, axis=-1, keepdims=True)
    return xv * lax.rsqrt(ms + EPS) * (1.0 + scale) + shift


def _dot(a, b):
    return jnp.dot(a, b, preferred_element_type=F32)


def _group_meansq(z, g):
    z2 = z * z
    hi = z2.astype(BF16)
    lo = (z2 - hi.astype(F32)).astype(BF16)
    return _dot(hi, g) + _dot(lo, g)


def _group_rsqrt(z, g):
    w = g.shape[0]
    parts = [lax.rsqrt(_group_meansq(z[:, c:c + w], g) + EPS) for c in range(0, z.shape[1], w)]
    return jnp.concatenate(parts, axis=1) if len(parts) > 1 else parts[0]


def _ada_kernel(c_ref, w_ref, b_ref, o_ref):
    cv = c_ref[...]
    s = cv * _sigmoid(cv)
    o_ref[0] = _dot(s, w_ref[0]) + b_ref[0]


def _ada_call(cs, ada_w, ada_b):
    depth, d, n = ada_w.shape
    tn = n // 4
    return pl.pallas_call(
        _ada_kernel,
        out_shape=jax.ShapeDtypeStruct((depth, cs.shape[0], n), F32),
        grid=(depth, n // tn),
        in_specs=[pl.BlockSpec(cs.shape, lambda l, j: (0, 0)),
                  pl.BlockSpec((1, d, tn), lambda l, j: (l, 0, j)),
                  pl.BlockSpec((1, 1, tn), lambda l, j: (l, 0, j))],
        out_specs=pl.BlockSpec((1, cs.shape[0], tn), lambda l, j: (l, 0, j)),
        compiler_params=_cparams(2),
        name="ada_mod",
    )(cs, ada_w, ada_b.reshape(depth, 1, n))


def _pre_ret_kernel(x_ref, mod_ref, cs_ref, w_ref, wp_ref, q_ref, k_ref, v_ref, g_ref):
    mod = mod_ref[0]
    h = _modulate(x_ref[0], mod[0:1], mod[1:2]).astype(BF16)
    dk = q_ref.shape[2] // RET_HEADS
    nqk = RET_HEADS * dk
    nv = v_ref.shape[2]
    cos = cs_ref[:, 0:dk]
    sin = cs_ref[:, dk:2 * dk]
    zq = _dot(h, w_ref[:, 0:nqk])
    zk = _dot(h, w_ref[:, nqk:2 * nqk])
    o2 = 2 * nqk + 2 * nv
    zqp = _dot(h, wp_ref[:, 0:nqk])
    zkp = _dot(h, wp_ref[:, nqk:2 * nqk])
    kscale = dk ** -0.5
    for hh in range(RET_HEADS):
        sl = slice(hh * dk, (hh + 1) * dk)
        q_ref[0, :, sl] = (zq[:, sl] * cos + zqp[:, sl] * sin).astype(BF16)
        k_ref[0, :, sl] = ((zk[:, sl] * cos + zkp[:, sl] * sin) * kscale).astype(BF16)
    v_ref[0] = _dot(h, w_ref[:, 2 * nqk:2 * nqk + nv]).astype(BF16)
    g = _dot(h, w_ref[:, 2 * nqk + nv:o2])
    g_ref[0] = (g * _sigmoid(g)).astype(BF16)


def _pre_diff_kernel(x_ref, mod_ref, cs_ref, w_ref, wp_ref, gq_ref, gk_ref, gm_ref,
                     qa_ref, qb_ref, k_ref, v_ref, *, qscale):
    mod = mod_ref[0]
    h = _modulate(x_ref[0], mod[0:1], mod[1:2]).astype(BF16)
    n = k_ref.shape[2]
    tm = x_ref.shape[1]
    cos = cs_ref[:, 0:LANES]
    sin = cs_ref[:, LANES:2 * LANES]
    gq = gq_ref[...]
    gk = gk_ref[...]
    qc, qs = gq[0:1] * cos * qscale, gq[1:2] * sin * qscale
    kc, ks = gk[0:1] * cos, gk[1:2] * sin
    gm = gm_ref[...]
    first = lax.broadcasted_iota(jnp.int32, (tm, LANES), 1) < LANES // 2
    zq = _dot(h, w_ref[:, 0:n])
    zk = _dot(h, w_ref[:, n:2 * n])
    zqp = _dot(h, wp_ref[:, 0:n])
    zkp = _dot(h, wp_ref[:, n:2 * n])
    rq = _group_rsqrt(zq, gm)
    rk = _group_rsqrt(zk, gm)
    for hh in range(n // LANES):
        sl = slice(hh * LANES, (hh + 1) * LANES)
        q = rq[:, sl] * (zq[:, sl] * qc + zqp[:, sl] * qs)
        qa_ref[0, :, sl] = jnp.where(first, q, 0.0).astype(BF16)
        qb_ref[0, :, sl] = jnp.where(first, 0.0, q).astype(BF16)
        k = rk[:, sl] * (zk[:, sl] * kc + zkp[:, sl] * ks)
        k_ref[0, :, sl] = k.astype(BF16)
    v_ref[0] = _dot(h, w_ref[:, 2 * n:3 * n]).astype(BF16)


def _pre_na_kernel(x_ref, mod_ref, w_ref, gq_ref, gk_ref, gm_ref,
                   q_ref, k_ref, v_ref, *, qscale):
    mod = mod_ref[0]
    h = _modulate(x_ref[0], mod[0:1], mod[1:2]).astype(BF16)
    n = k_ref.shape[2]
    gq = gq_ref[...] * qscale
    gk = gk_ref[...]
    gm = gm_ref[...]
    zq = _dot(h, w_ref[:, 0:n])
    zk = _dot(h, w_ref[:, n:2 * n])
    rq = _group_rsqrt(zq, gm)
    rk = _group_rsqrt(zk, gm)
    for hh in range(n // LANES):
        sl = slice(hh * LANES, (hh + 1) * LANES)
        q_ref[0, :, sl] = (zq[:, sl] * rq[:, sl] * gq).astype(BF16)
        k_ref[0, :, sl] = (zk[:, sl] * rk[:, sl] * gk).astype(BF16)
    v_ref[0] = _dot(h, w_ref[:, 2 * n:3 * n]).astype(BF16)


def _pre_mla_kernel(x_ref, mod_ref, cs_ref, wd_ref, gqr_ref, gkv_ref, wuq_ref, wukv_ref,
                    gq_ref, gk_ref, gmq_ref, gmk_ref, q_ref, k_ref, v_ref, *, qscale):
    mod = mod_ref[0]
    h = _modulate(x_ref[0], mod[0:1], mod[1:2]).astype(BF16)
    tm = x_ref.shape[1]
    n = q_ref.shape[2]
    z = _dot(h, wd_ref[...])
    o1 = MLA_Q_RANK
    o2 = o1 + MLA_KV_RANK
    zq = z[:, 0:o1]
    qn = (zq * lax.rsqrt(jnp.mean(zq * zq, axis=-1, keepdims=True) + EPS) * gqr_ref[...]).astype(BF16)
    zc = z[:, o1:o2]
    cn = (zc * lax.rsqrt(jnp.mean(zc * zc, axis=-1, keepdims=True) + EPS) * gkv_ref[...]).astype(BF16)
    zr = z[:, o2:o2 + LANES]
    zrp = z[:, o2 + LANES:o2 + 2 * LANES]
    cos = cs_ref[:, 0:LANES]
    sin = cs_ref[:, LANES:2 * LANES]
    gq = gq_ref[...]
    gk = gk_ref[...]
    qc, qs = gq[0:1] * cos * qscale, gq[1:2] * sin * qscale
    msr = jnp.sum(zr * zr, axis=-1, keepdims=True) * (1.0 / MLA_ROPE)
    krope = lax.rsqrt(msr + EPS) * (zr * (gk[0:1] * cos) + zrp * (gk[1:2] * sin))
    gmq = gmq_ref[...]
    gmk = gmk_ref[...]
    one_lane = (lax.broadcasted_iota(jnp.int32, (tm, LANES), 1) == MLA_V).astype(F32)
    uq = _dot(qn, wuq_ref[:, 0:n])
    uqp = _dot(qn, wuq_ref[:, n:2 * n])
    uk = _dot(cn, wukv_ref[:, 0:n])
    uv = _dot(cn, wukv_ref[:, n:2 * n])
    rq = _group_rsqrt(uq, gmq)
    rk = _group_rsqrt(uk, gmk)
    for hh in range(n // LANES):
        sl = slice(hh * LANES, (hh + 1) * LANES)
        q = rq[:, sl] * (uq[:, sl] * qc + uqp[:, sl] * qs)
        q_ref[0, :, sl] = q.astype(BF16)
        k = uk[:, sl] * rk[:, sl] * gk[0:1] + krope
        k_ref[0, :, sl] = k.astype(BF16)
        v_ref[0, :, sl] = (uv[:, sl] + one_lane).astype(BF16)


def _pre_call(kern, x, mod, tables, consts, out_widths, name):
    nb, n, d = x.shape
    tm = min(TM, n)
    assert n % tm == 0
    tok = lambda w: pl.BlockSpec((1, tm, w), lambda b, i: (b, i, 0))
    in_specs = [tok(d), pl.BlockSpec((1, ADA_CHUNKS, d), lambda b, i: (b, 0, 0))]
    in_specs += [pl.BlockSpec((tm, t.shape[1]), lambda b, i: (i, 0)) for t in tables]
    in_specs += [_const_spec(a) for a in consts]
    return pl.pallas_call(
        kern,
        out_shape=[jax.ShapeDtypeStruct((nb, n, w), BF16) for w in out_widths],
        grid=(nb, n // tm),
        in_specs=in_specs,
        out_specs=[tok(w) for w in out_widths],
        compiler_params=_cparams(2),
        name=name,
    )(x, mod, *tables, *consts)


def _scalar_vec(s):
    return jnp.full((1, 1), s, F32)


def _ret_lat_kernel(lg_ref, qf_ref, kf_ref, vf_ref, qb_ref, kb_ref, vb_ref, kc_ref, vc_ref,
                    of_ref, ob_ref, sf_s, sb_s):
    c = pl.program_id(1)
    cn = qf_ref.shape[1]
    n_ctx = kc_ref.shape[1]
    dk = qf_ref.shape[2] // RET_HEADS
    dv = vf_ref.shape[2] // RET_HEADS

    @pl.when(c == 0)
    def _():
        pos = lax.broadcasted_iota(jnp.int32, (n_ctx, 1), 0).astype(F32)
        for hh in range(RET_HEADS):
            lf = _scalar_vec(lg_ref[0, hh])
            lb = _scalar_vec(lg_ref[1, hh])
            kc = kc_ref[0, :, hh * dk:(hh + 1) * dk].astype(F32)
            vc = vc_ref[0, :, hh * dv:(hh + 1) * dv]
            wf = jnp.exp((n_ctx - 1.0 - pos) * lf)
            wb = jnp.exp(pos * lb)
            sf_s[hh] = lax.dot_general((kc * wf).astype(BF16), vc, _TN, preferred_element_type=F32)
            sb_s[hh] = lax.dot_general((kc * wb).astype(BF16), vc, _TN, preferred_element_type=F32)

    ii = lax.broadcasted_iota(jnp.int32, (cn, cn), 0)
    jj = lax.broadcasted_iota(jnp.int32, (cn, cn), 1)
    rel = (ii - jj).astype(F32)
    idx = lax.broadcasted_iota(jnp.int32, (cn, 1), 0).astype(F32)

    for hh in range(RET_HEADS):
        lf = _scalar_vec(lg_ref[0, hh])
        lb = _scalar_vec(lg_ref[1, hh])
        ksl = slice(hh * dk, (hh + 1) * dk)
        vsl = slice(hh * dv, (hh + 1) * dv)

        q, k, v = qf_ref[0, :, ksl], kf_ref[0, :, ksl], vf_ref[0, :, vsl]
        dec = jnp.where(rel >= 0, jnp.exp(jnp.maximum(rel, 0.0) * lf), 0.0)
        att = lax.dot_general(q, k, _NT, preferred_element_type=F32) * dec
        s = sf_s[hh]
        of_ref[0, :, vsl] = _dot(att.astype(BF16), v) + _dot(q, s.astype(BF16)) * jnp.exp((idx + 1.0) * lf)
        kd = (k.astype(F32) * jnp.exp((cn - 1.0 - idx) * lf)).astype(BF16)
        sf_s[hh] = s * jnp.exp(cn * lf) + lax.dot_general(kd, v, _TN, preferred_element_type=F32)

        q, k, v = qb_ref[0, :, ksl], kb_ref[0, :, ksl], vb_ref[0, :, vsl]
        dec = jnp.where(rel <= 0, jnp.exp(jnp.maximum(-rel, 0.0) * lb), 0.0)
        att = lax.dot_general(q, k, _NT, preferred_element_type=F32) * dec
        s = sb_s[hh]
        ob_ref[0, :, vsl] = _dot(att.astype(BF16), v) + _dot(q, s.astype(BF16)) * jnp.exp((cn - idx) * lb)
        kd = (k.astype(F32) * jnp.exp(idx * lb)).astype(BF16)
        sb_s[hh] = s * jnp.exp(cn * lb) + lax.dot_general(kd, v, _TN, preferred_element_type=F32)


def _ret_lat_call(lg, q, k, v, kc, vc):
    b, n, nqk = q.shape
    dk = nqk // RET_HEADS
    nv = v.shape[2]
    dv = nv // RET_HEADS
    n_ctx = kc.shape[1]
    cn = RET_CHUNK
    nc = n // cn
    fwd = lambda w: pl.BlockSpec((1, cn, w), lambda bb, c: (bb, c, 0))
    bwd = lambda w: pl.BlockSpec((1, cn, w), lambda bb, c: (bb, nc - 1 - c, 0))
    ctx = lambda w: pl.BlockSpec((1, n_ctx, w), lambda bb, c: (bb, 0, 0))
    return pl.pallas_call(
        _ret_lat_kernel,
        out_shape=[jax.ShapeDtypeStruct((b, n, nv), F32)] * 2,
        grid=(b, nc),
        in_specs=[pl.BlockSpec(memory_space=pltpu.SMEM),
                  fwd(nqk), fwd(nqk), fwd(nv), bwd(nqk), bwd(nqk), bwd(nv), ctx(nqk), ctx(nv)],
        out_specs=[fwd(nv), bwd(nv)],
        scratch_shapes=[pltpu.VMEM((RET_HEADS, dk, dv), F32), pltpu.VMEM((RET_HEADS, dk, dv), F32)],
        compiler_params=_cparams(2),
        name="ret_lat",
    )(lg, q, k, v, q, k, v, kc, vc)


def _ret_ctx_kernel(lg_ref, q_ref, k_ref, v_ref, o_ref):
    hh = pl.program_id(1)
    lf = _scalar_vec(lg_ref[0, hh])
    lb = _scalar_vec(lg_ref[1, hh])
    n = q_ref.shape[1]
    ii = lax.broadcasted_iota(jnp.int32, (n, n), 0)
    jj = lax.broadcasted_iota(jnp.int32, (n, n), 1)
    rel = (ii - jj).astype(F32)
    dec = (jnp.where(rel >= 0, jnp.exp(jnp.maximum(rel, 0.0) * lf), 0.0)
           + jnp.where(rel <= 0, jnp.exp(jnp.maximum(-rel, 0.0) * lb), 0.0))
    att = lax.dot_general(q_ref[0], k_ref[0], _NT, preferred_element_type=F32) * dec
    o_ref[0] = _dot(att.astype(BF16), v_ref[0])


def _ret_ctx_call(lg, q, k, v):
    b, n, nqk = q.shape
    dk = nqk // RET_HEADS
    dv = v.shape[2] // RET_HEADS
    spec = lambda w: pl.BlockSpec((1, n, w), lambda bb, hh: (bb, 0, hh))
    return pl.pallas_call(
        _ret_ctx_kernel,
        out_shape=jax.ShapeDtypeStruct((b, n, RET_HEADS * dv), F32),
        grid=(b, RET_HEADS),
        in_specs=[pl.BlockSpec(memory_space=pltpu.SMEM), spec(dk), spec(dk), spec(dv)],
        out_specs=spec(dv),
        compiler_params=_cparams(2),
        name="ret_ctx",
    )(lg, q, k, v)


def _lane_tiles(v, n):
    return jnp.concatenate([v] * n, axis=1) if n > 1 else v


def _softmax_update(s, m_s, l_s):
    n = s.shape[1] // LANES
    m_prev = m_s[...]
    m_new = jnp.maximum(m_prev, jnp.max(s, axis=-1, keepdims=True))
    alpha = jnp.exp2(m_prev - m_new)
    m_s[...] = m_new
    p = jnp.exp2(s - _lane_tiles(m_new, n))
    if l_s is not None:
        part = p[:, 0:LANES]
        for t in range(1, n):
            part = part + p[:, t * LANES:(t + 1) * LANES]
        l_s[...] = alpha * l_s[...] + part
    return alpha, p


def _flash_streams(streams, kl_ref, vl_ref, kc_ref, vc_ref, n_main, lanes=slice(None)):
    for q, m_s, l_s, acc_s, _, _, _ in streams:
        m_s[...] = jnp.full(m_s.shape, NEG, F32)
        acc_s[...] = jnp.zeros(acc_s.shape, F32)
        if l_s is not None:
            l_s[...] = jnp.zeros(l_s.shape, F32)
        s = lax.dot_general(q, kc_ref[0, :, lanes], _NT, preferred_element_type=F32)
        alpha, p = _softmax_update(s, m_s, l_s)
        acc_s[...] = alpha * acc_s[...] + _dot(p.astype(BF16), vc_ref[0, :, lanes])
    if not n_main:
        return
    assert n_main >= 2 and n_main % 2 == 0

    def chunk(ref, j):
        off = j * TK if isinstance(j, int) else pl.multiple_of(j * TK, TK)
        return ref[0, pl.ds(off, TK), lanes]

    def scores(j, slot):
        k_c = chunk(kl_ref, j)
        for q, _, _, _, s_scr, _, _ in streams:
            s_scr[slot] = lax.dot_general(q, k_c, _NT, preferred_element_type=F32)

    def softmax(slot):
        for _, m_s, l_s, _, s_scr, p_scr, al_scr in streams:
            alpha, p = _softmax_update(s_scr[slot], m_s, l_s)
            al_scr[slot] = alpha
            p_scr[slot] = p.astype(BF16)

    def values(j, slot):
        v_c = chunk(vl_ref, j)
        for _, _, _, acc_s, _, p_scr, al_scr in streams:
            acc_s[...] = al_scr[slot] * acc_s[...] + _dot(p_scr[slot], v_c)

    scores(0, 0)
    scores(1, 1)
    softmax(0)

    def step(j, par):
        scores(j + 2, par)
        softmax(1 - par)
        values(j, par)

    n_pipe = n_main - 2
    n_loop = n_pipe // FLASH_UNROLL

    def body(t, carry):
        for u in range(FLASH_UNROLL):
            step(t * FLASH_UNROLL + u, u % 2)
        return carry

    if n_loop:
        lax.fori_loop(0, n_loop, body, 0)
    for j in range(n_loop * FLASH_UNROLL, n_pipe):
        step(j, j % 2)
    softmax(1)
    values(n_main - 2, 0)
    values(n_main - 1, 1)


def _stream_scratch(tq, n_main, with_l):
    stat = [pltpu.VMEM((tq, LANES), F32)] * (3 if with_l else 2)
    if not n_main:
        return stat
    return stat + [pltpu.VMEM((2, tq, TK), F32), pltpu.VMEM((2, tq, TK), BF16), pltpu.VMEM((2, tq, LANES), F32)]


def _mla_flash_kernel(*refs, n_main):
    if n_main:
        q_ref, kl_ref, vl_ref, kc_ref, vc_ref, o_ref, m_s, acc_s, s_scr, p_scr, al_scr = refs
    else:
        q_ref, kc_ref, vc_ref, o_ref, m_s, acc_s = refs
        kl_ref = vl_ref = s_scr = p_scr = al_scr = None
    for hh in range(q_ref.shape[2] // LANES):
        lanes = slice(hh * LANES, (hh + 1) * LANES)
        _flash_streams([(q_ref[0, :, lanes], m_s, None, acc_s, s_scr, p_scr, al_scr)],
                       kl_ref, vl_ref, kc_ref, vc_ref, n_main, lanes)
        acc = acc_s[...]
        o_ref[0, :, lanes] = (acc / acc[:, MLA_V:MLA_V + 1]).astype(BF16)


def _mla_flash_call(q, k_ctx, v_ctx, k_lat=None, v_lat=None):
    b, nq, w = q.shape
    nh = w // LANES
    n_ctx = k_ctx.shape[1]
    tq = min(TQ, nq)
    assert nq % tq == 0
    gw = MLA_HEADS_PER_STEP * LANES if nh % MLA_HEADS_PER_STEP == 0 else LANES
    nh = w // gw
    qspec = pl.BlockSpec((1, tq, gw), lambda bb, hh, i: (bb, i, hh))
    full = lambda n: pl.BlockSpec((1, n, gw), lambda bb, hh, i: (bb, 0, hh))
    args, in_specs, n_main = [q], [qspec], 0
    if k_lat is not None:
        n_lat = k_lat.shape[1]
        assert n_lat % TK == 0
        n_main = n_lat // TK
        args += [k_lat, v_lat]
        in_specs += [full(n_lat), full(n_lat)]
    args += [k_ctx, v_ctx]
    in_specs += [full(n_ctx), full(n_ctx)]
    return pl.pallas_call(
        functools.partial(_mla_flash_kernel, n_main=n_main),
        out_shape=jax.ShapeDtypeStruct((b, nq, w), BF16),
        grid=(b, nh, nq // tq),
        in_specs=in_specs,
        out_specs=qspec,
        scratch_shapes=_stream_scratch(tq, n_main, with_l=False),
        compiler_params=_cparams(3),
        name="mla_attn" if n_main else "mla_attn_ctx",
    )(*args)


def _diff_flash_kernel(*refs, n_main, out_scale):
    if n_main:
        (lam_ref, qa_ref, qb_ref, kl_ref, vl_ref, kc_ref, vc_ref, g_ref, o_ref,
         m1, l1, a1, m2, l2, a2, s_scr, p_scr, al_scr) = refs
    else:
        (lam_ref, qa_ref, qb_ref, kc_ref, vc_ref, g_ref, o_ref, m1, l1, a1, m2, l2, a2) = refs
        kl_ref = vl_ref = s_scr = p_scr = al_scr = None
    for q_ref, m_s, l_s, a_s in ((qa_ref, m1, l1, a1), (qb_ref, m2, l2, a2)):
        _flash_streams([(q_ref[0], m_s, l_s, a_s, s_scr, p_scr, al_scr)], kl_ref, vl_ref, kc_ref, vc_ref, n_main)
    rowsum = lambda l_s: jnp.sum(l_s[...], axis=-1, keepdims=True)
    o = a1[...] / rowsum(l1) - lam_ref[0] * (a2[...] / rowsum(l2))
    ms = jnp.mean(o * o, axis=-1, keepdims=True)
    o_ref[0] = (o * lax.rsqrt(ms + EPS) * (g_ref[...] * out_scale)).astype(BF16)


def _diff_flash_call(lam, subln_g, out_scale, qa, qb, k_ctx, v_ctx, k_lat=None, v_lat=None):
    b, nq, w = qa.shape
    nh = w // LANES
    n_ctx = k_ctx.shape[1]
    tq = min(TQ, nq)
    assert nq % tq == 0
    qspec = pl.BlockSpec((1, tq, LANES), lambda bb, hh, i: (bb, i, hh))
    full = lambda n: pl.BlockSpec((1, n, LANES), lambda bb, hh, i: (bb, 0, hh))
    args = [lam, qa, qb]
    in_specs = [pl.BlockSpec(memory_space=pltpu.SMEM), qspec, qspec]
    n_main = 0
    if k_lat is not None:
        n_lat = k_lat.shape[1]
        assert n_lat % TK == 0
        n_main = n_lat // TK
        args += [k_lat, v_lat]
        in_specs += [full(n_lat), full(n_lat)]
    args += [k_ctx, v_ctx, subln_g]
    in_specs += [full(n_ctx), full(n_ctx), pl.BlockSpec((1, LANES), lambda bb, hh, i: (0, 0))]
    stat = _stream_scratch(tq, 0, with_l=True)
    stage = _stream_scratch(tq, n_main, with_l=True)[len(stat):]
    return pl.pallas_call(
        functools.partial(_diff_flash_kernel, n_main=n_main, out_scale=out_scale),
        out_shape=jax.ShapeDtypeStruct((b, nq, w), BF16),
        grid=(b, nh, nq // tq),
        in_specs=in_specs,
        out_specs=qspec,
        scratch_shapes=stat + stat + stage,
        compiler_params=_cparams(3),
        name="diff_attn" if n_main else "diff_attn_ctx",
    )(*args)


def _na_kernel(*refs, n_loc):
    q_ref = refs[0]
    k_loc = refs[1:1 + n_loc]
    v_loc = refs[1 + n_loc:1 + 2 * n_loc]
    rest = refs[1 + 2 * n_loc:]
    if n_loc:
        tab_ref, kc_ref, vc_ref, o_ref = rest
    else:
        kc_ref, vc_ref, o_ref = rest
    q = q_ref[0]
    tq = q.shape[0]
    first = lax.broadcasted_iota(jnp.int32, (tq, LANES), 1) < LANES // 2
    zero = jnp.zeros_like(q)
    outs = []
    for hh in range(2):
        qh = jnp.where(first, q, zero) if hh == 0 else jnp.where(first, zero, q)
        ss = []
        for j in range(n_loc):
            tkb = k_loc[j].shape[1]
            s = lax.dot_general(qh, k_loc[j][0], _NT, preferred_element_type=F32)
            ss.append(s + tab_ref[0, hh, :, j * tkb:(j + 1) * tkb])
        ss.append(lax.dot_general(qh, kc_ref[0], _NT, preferred_element_type=F32))
        m = functools.reduce(jnp.maximum, [jnp.max(s, axis=-1, keepdims=True) for s in ss])
        ps = [jnp.exp2(s - m) for s in ss]
        l = functools.reduce(lambda a, c: a + c, [jnp.sum(p, axis=-1, keepdims=True) for p in ps])
        vs = [r[0] for r in v_loc] + [vc_ref[0]]
        o = functools.reduce(lambda a, c: a + c, [_dot(p.astype(BF16), v) for p, v in zip(ps, vs)])
        outs.append(o / l)
    o_ref[0] = jnp.where(first, outs[0], outs[1]).astype(BF16)


def _na_call(q, k_ctx, v_ctx, k_lat=None, v_lat=None, table=None):
    b, nq, w = q.shape
    ng = w // LANES
    n_ctx = k_ctx.shape[1]
    if k_lat is None:
        tq, nblk, n_loc = nq, 1, 0
    else:
        tq = NA_ROWS * GRID_W
        nblk = nq // tq
        n_loc = 3
        assert nq % tq == 0 and nblk >= 3
    qspec = pl.BlockSpec((1, tq, LANES), lambda g, bb, i: (bb, i, g))
    prev = pl.BlockSpec((1, tq, LANES), lambda g, bb, i: (bb, jnp.maximum(i - 1, 0), g))
    nxt = pl.BlockSpec((1, tq, LANES), lambda g, bb, i: (bb, jnp.minimum(i + 1, nblk - 1), g))
    cspec = pl.BlockSpec((1, n_ctx, LANES), lambda g, bb, i: (bb, 0, g))
    args, in_specs = [q], [qspec]
    if n_loc:
        args += [k_lat] * 3 + [v_lat] * 3 + [table]
        variant = lambda i: jnp.where(i == 0, 0, jnp.where(i == nblk - 1, 2, 1))
        in_specs += [prev, qspec, nxt] * 2
        in_specs += [pl.BlockSpec((1, 2, tq, 3 * tq), lambda g, bb, i: (variant(i), g, 0, 0))]
    args += [k_ctx, v_ctx]
    in_specs += [cspec, cspec]
    return pl.pallas_call(
        functools.partial(_na_kernel, n_loc=n_loc),
        out_shape=jax.ShapeDtypeStruct((b, nq, w), BF16),
        grid=(ng, b, nblk),
        in_specs=in_specs,
        out_specs=qspec,
        compiler_params=_cparams(3),
        name="na_attn" if n_loc else "na_attn_ctx",
    )(*args)


def _na_table(rel_bias, rows):
    r_blk = NA_ROWS
    nblk = rows // r_blk
    wh = min(NA_WIN_H, rows)
    col = np.arange(GRID_W)
    c0 = np.clip(col - NA_WIN_W // 2, 0, GRID_W - NA_WIN_W)
    dcol = col[None, :] - col[:, None]
    col_ok = (col[None, :] >= c0[:, None]) & (col[None, :] < c0[:, None] + NA_WIN_W)
    dc_idx = np.clip(dcol + NA_WIN_W - 1, 0, 2 * NA_WIN_W - 2)
    nh = rel_bias.shape[0]
    t1 = jnp.where(col_ok[None, None], rel_bias[:, :, dc_idx] * LOG2E, NEG)
    t1 = jnp.concatenate([t1, jnp.full((nh, 1, GRID_W, GRID_W), NEG, F32)], axis=1)
    masked = 2 * NA_WIN_H - 1
    idx = np.full((3, r_blk, 3 * r_blk), masked, np.int32)
    for v, i in enumerate((0, 1, nblk - 1)):
        for qr in range(r_blk):
            r = i * r_blk + qr
            r0 = min(max(r - NA_WIN_H // 2, 0), rows - wh)
            for slot, blk in enumerate((i - 1, i, i + 1)):
                if blk < 0 or blk >= nblk:
                    continue
                for kr_l in range(r_blk):
                    kr = blk * r_blk + kr_l
                    if r0 <= kr < r0 + wh:
                        idx[v, qr, slot * r_blk + kr_l] = kr - r + NA_WIN_H - 1
    tab = t1[:, idx]
    tab = jnp.transpose(tab, (1, 0, 2, 4, 3, 5))
    return tab.reshape(3, nh, r_blk * GRID_W, 3 * r_blk * GRID_W)


def _post_kernel(*refs, n_o, ret):
    o_refs = refs[:n_o]
    refs = refs[n_o:]
    if ret:
        gs_ref, ng_ref = refs[:2]
        refs = refs[2:]
    x_ref, mod_ref, wo_ref, w1_ref, w2_ref, out_ref = refs
    mod = mod_ref[0]
    if ret:
        o = functools.reduce(lambda a, c: a + c, [r[0] for r in o_refs])
        dv = o.shape[1] // RET_HEADS
        y = None
        for hh in range(RET_HEADS):
            sl = slice(hh * dv, (hh + 1) * dv)
            oh = o[:, sl]
            d = oh - jnp.mean(oh, axis=-1, keepdims=True)
            var = jnp.mean(d * d, axis=-1, keepdims=True)
            on = d * lax.rsqrt(var + EPS) * ng_ref[:, sl] * gs_ref[0, :, sl].astype(F32)
            part = _dot(on.astype(BF16), wo_ref[sl, :])
            y = part if y is None else y + part
    else:
        y = _dot(o_refs[0][0], wo_ref[...])
    x1 = x_ref[0] + mod[2:3] * y
    h2 = _modulate(x1, mod[3:4], mod[4:5]).astype(BF16)
    fh = w2_ref.shape[0]
    ua = _dot(h2, w1_ref[:, 0:fh])
    ug = _dot(h2, w1_ref[:, fh:2 * fh])
    act = (ua * _sigmoid(ua) * ug).astype(BF16)
    out_ref[0] = x1 + mod[5:6] * _dot(act, w2_ref[...])


def _post_call(os_, x, mod, wo, w1, w2, gs=None, ng=None, name="post"):
    nb, n, d = x.shape
    tm = min(TM, n)
    assert n % tm == 0
    tok = lambda w: pl.BlockSpec((1, tm, w), lambda b, i: (b, i, 0))
    ret = gs is not None
    args = list(os_)
    in_specs = [tok(o.shape[2]) for o in os_]
    if ret:
        args += [gs, ng]
        in_specs += [tok(gs.shape[2]), _const_spec(ng)]
    args += [x, mod, wo, w1, w2]
    in_specs += [tok(d), pl.BlockSpec((1, ADA_CHUNKS, d), lambda b, i: (b, 0, 0)),
                 _const_spec(wo), _const_spec(w1), _const_spec(w2)]
    return pl.pallas_call(
        functools.partial(_post_kernel, n_o=len(os_), ret=ret),
        out_shape=jax.ShapeDtypeStruct((nb, n, d), F32),
        grid=(nb, n // tm),
        in_specs=in_specs,
        out_specs=tok(d),
        compiler_params=_cparams(2),
        name=name,
    )(*args)


def _rope_cos_sin(n, d):
    quarter = d // 4
    rows = n // GRID_W
    inv_freq = jnp.power(ROPE_BASE, -jnp.arange(quarter, dtype=F32) / quarter)
    ar = jnp.arange(rows, dtype=F32)[:, None] * inv_freq[None, :]
    ac = jnp.arange(GRID_W, dtype=F32)[:, None] * inv_freq[None, :]
    by_row = lambda t: jnp.broadcast_to(t[:, None, :], (rows, GRID_W, quarter)).reshape(n, quarter)
    by_col = lambda t: jnp.broadcast_to(t[None, :, :], (rows, GRID_W, quarter)).reshape(n, quarter)
    cr, sr, cc, sc = by_row(jnp.cos(ar)), by_row(jnp.sin(ar)), by_col(jnp.cos(ac)), by_col(jnp.sin(ac))
    cos = jnp.concatenate([cr, cr, cc, cc], axis=-1)
    sin = jnp.concatenate([-sr, sr, -sc, sc], axis=-1)
    return cos, sin


def _rope_partner(d):
    half, quarter = d // 2, d // 4
    l = np.arange(d)
    return np.where((l % half) < quarter, l + quarter, l - quarter)


def _tiled_partner(d, n):
    return (np.arange(n) // d) * d + _rope_partner(d)[np.arange(n) % d]


V7X_MXU_WIDTH = 256


def _group_matrix(blocks):
    g = np.zeros((V7X_MXU_WIDTH, V7X_MXU_WIDTH), np.float32)
    for base in range(0, V7X_MXU_WIDTH, LANES):
        for a, b in blocks:
            g[base + a:base + b, base + a:base + b] = 1.0 / (b - a)
    return jnp.asarray(g, BF16)


def _identity_tables(n, w):
    return jnp.concatenate([jnp.ones((n, w), F32), jnp.zeros((n, w), F32)], axis=-1)


def kernel(x, c, ctx, c_ctx, ada_w, ada_b, ret_w_in, ret_decay_logit, ret_norm_g, ret_w_out, diff_w_in, diff_q_norm_g, diff_k_norm_g, diff_lambda, diff_subln_g, diff_w_out, na_w_in, na_q_norm_g, na_k_norm_g, na_rel_bias, na_w_out, mla_w_down, mla_q_norm_g, mla_kv_norm_g, mla_w_uq, mla_w_ukv, mla_qk_norm_q, mla_qk_norm_k, mla_w_out, ffn_w_in, ffn_w_out):
    bsz, seq, d = x.shape
    n_ctx = ctx.shape[1]
    depth = ada_w.shape[0]
    assert seq % GRID_W == 0 and bsz + 1 <= 8

    cs = jnp.zeros((8, d), F32).at[:bsz].set(c).at[bsz].set(c_ctx)
    mods = _ada_call(cs, ada_w, ada_b)

    x_lat = x
    x_ctx = ctx.reshape(1, bsz * n_ctx, d)
    as_ctx = lambda a: a.reshape(bsz, n_ctx, a.shape[-1])
    g64 = _group_matrix([(0, 64), (64, 128)])

    for i in range(depth):
        need_ctx = i < depth - 1
        mod_l = mods[i, :bsz].reshape(bsz, ADA_CHUNKS, d)
        mod_c = mods[i, bsz:bsz + 1].reshape(1, ADA_CHUNKS, d)
        kind, j = i % N_MIXERS, i // N_MIXERS
        gs_l = gs_c = ng = None

        if kind == 0:
            w = ret_w_in[j]
            dk = d // RET_HEADS
            nqk = RET_HEADS * dk
            perm = _tiled_partner(dk, nqk)
            wcat = [w.astype(BF16), w[:, np.concatenate([perm, nqk + perm])].astype(BF16)]
            cos, sin = _rope_cos_sin(seq, dk)
            tab_l = jnp.concatenate([cos, sin], axis=-1)
            tab_c = _identity_tables(bsz * n_ctx, dk)
            widths = [nqk, nqk, 2 * nqk, 2 * nqk]
            ql, kl, vl, gs_l = _pre_call(_pre_ret_kernel, x_lat, mod_l, [tab_l], wcat, widths, "pre_ret")
            qc, kc, vc, gs_c = _pre_call(_pre_ret_kernel, x_ctx, mod_c, [tab_c], wcat, widths, "pre_ret_ctx")
            qc, kc, vc = as_ctx(qc), as_ctx(kc), as_ctx(vc)
            lg = jax.nn.log_sigmoid(ret_decay_logit[j].astype(F32))
            o_l = list(_ret_lat_call(lg, ql, kl, vl, kc, vc))
            o_c = [_ret_ctx_call(lg, qc, kc, vc).reshape(1, bsz * n_ctx, -1)] if need_ctx else None
            ng = ret_norm_g[j].reshape(1, -1)
            wo = ret_w_out[j].astype(BF16)
        elif kind == 1:
            w = diff_w_in[j]
            dh = d // (2 * DIFF_HEADS)
            nq = 2 * DIFF_HEADS * dh
            perm = _tiled_partner(dh, nq)
            wcat = [w.astype(BF16), w[:, np.concatenate([perm, nq + perm])].astype(BF16)]
            cos, sin = _rope_cos_sin(seq, dh)
            tab_l = jnp.concatenate([cos, cos, sin, sin], axis=-1)
            tab_c = _identity_tables(bsz * n_ctx, LANES)
            pl_ = _rope_partner(dh)
            gq = jnp.stack([jnp.tile(diff_q_norm_g[j], 2), jnp.tile(diff_q_norm_g[j][pl_], 2)])
            gk = jnp.stack([jnp.tile(diff_k_norm_g[j], 2), jnp.tile(diff_k_norm_g[j][pl_], 2)])
            kern = functools.partial(_pre_diff_kernel, qscale=dh ** -0.5 * LOG2E)
            widths = [nq, nq, nq, nq]
            qa, qb, kl, vl = _pre_call(kern, x_lat, mod_l, [tab_l], wcat + [gq, gk, g64], widths, "pre_diff")
            qac, qbc, kc, vc = _pre_call(kern, x_ctx, mod_c, [tab_c], wcat + [gq, gk, g64], widths, "pre_diff_ctx")
            qac, qbc, kc, vc = as_ctx(qac), as_ctx(qbc), as_ctx(kc), as_ctx(vc)
            lv = diff_lambda[j].astype(F32)
            lambda_init = 0.8 - 0.6 * math.exp(-0.3 * i)
            lam = (jnp.exp(jnp.sum(lv[0] * lv[1])) - jnp.exp(jnp.sum(lv[2] * lv[3])) + lambda_init).reshape(1)
            sg = diff_subln_g[j].reshape(1, -1)
            o_l = [_diff_flash_call(lam, sg, 1.0 - lambda_init, qa, qb, kc, vc, kl, vl)]
            o_c = ([_diff_flash_call(lam, sg, 1.0 - lambda_init, qac, qbc, kc, vc).reshape(1, bsz * n_ctx, -1)]
                   if need_ctx else None)
            wo = diff_w_out[j].astype(BF16)
        elif kind == 2:
            w = na_w_in[j].astype(BF16)
            dh = d // NA_HEADS
            gq = jnp.tile(na_q_norm_g[j], LANES // dh).reshape(1, LANES)
            gk = jnp.tile(na_k_norm_g[j], LANES // dh).reshape(1, LANES)
            kern = functools.partial(_pre_na_kernel, qscale=dh ** -0.5 * LOG2E)
            widths = [d, d, d]
            ql, kl, vl = _pre_call(kern, x_lat, mod_l, [], [w, gq, gk, g64], widths, "pre_na")
            qc, kc, vc = _pre_call(kern, x_ctx, mod_c, [], [w, gq, gk, g64], widths, "pre_na_ctx")
            qc, kc, vc = as_ctx(qc), as_ctx(kc), as_ctx(vc)
            table = _na_table(na_rel_bias[j].astype(F32), seq // GRID_W)
            o_l = [_na_call(ql, kc, vc, kl, vl, table)]
            o_c = [_na_call(qc, kc, vc).reshape(1, bsz * n_ctx, -1)] if need_ctx else None
            wo = na_w_out[j].astype(BF16)
        else:
            nh = MLA_HEADS
            qk = MLA_NOPE + MLA_ROPE
            wd = mla_w_down[j]
            o2 = MLA_Q_RANK + MLA_KV_RANK
            rope_cols = wd[:, o2:o2 + MLA_ROPE]
            pad = lambda a, lo, hi: jnp.pad(a, ((0, 0), (lo, hi)))
            rope_grp = pad(rope_cols, MLA_NOPE, LANES - qk)
            rope_grp_p = pad(rope_cols[:, _rope_partner(MLA_ROPE)], MLA_NOPE, LANES - qk)
            wdcat = jnp.concatenate([wd[:, :o2], rope_grp, rope_grp_p], axis=1).astype(BF16)

            def head_pad(a, width):
                r = a.shape[0]
                return jnp.pad(a.reshape(r, nh, width), ((0, 0), (0, 0), (0, LANES - width))).reshape(r, nh * LANES)

            wuq = mla_w_uq[j]
            perm = (np.arange(nh * qk) // qk) * qk + np.concatenate(
                [np.arange(MLA_NOPE), MLA_NOPE + _rope_partner(MLA_ROPE)])[np.arange(nh * qk) % qk]
            wuqcat = jnp.concatenate([head_pad(wuq, qk), head_pad(wuq[:, perm], qk)], axis=1).astype(BF16)
            wukv = mla_w_ukv[j].reshape(MLA_KV_RANK, nh, MLA_NOPE + MLA_V)
            wk = head_pad(wukv[:, :, :MLA_NOPE].reshape(MLA_KV_RANK, -1), MLA_NOPE)
            wv = head_pad(wukv[:, :, MLA_NOPE:].reshape(MLA_KV_RANK, -1), MLA_V)
            wukvcat = jnp.concatenate([wk, wv], axis=1).astype(BF16)

            def gain_rows(g):
                gp = jnp.concatenate([g[:MLA_NOPE], g[MLA_NOPE:][_rope_partner(MLA_ROPE)]])
                return jnp.stack([jnp.pad(g, (0, LANES - qk)), jnp.pad(gp, (0, LANES - qk))])

            cos, sin = _rope_cos_sin(seq, MLA_ROPE)
            ones_l = lambda n, w_: jnp.ones((n, w_), F32)
            zeros_l = lambda n, w_: jnp.zeros((n, w_), F32)
            tab_l = jnp.concatenate([ones_l(seq, MLA_NOPE), cos, ones_l(seq, LANES - qk),
                                     zeros_l(seq, MLA_NOPE), sin, zeros_l(seq, LANES - qk)], axis=-1)
            tab_c = _identity_tables(bsz * n_ctx, LANES)
            gmq = _group_matrix([(0, MLA_NOPE), (MLA_NOPE, qk)])
            gmk = _group_matrix([(0, MLA_NOPE)])
            consts = [wdcat, mla_q_norm_g[j].reshape(1, -1), mla_kv_norm_g[j].reshape(1, -1), wuqcat, wukvcat,
                      gain_rows(mla_qk_norm_q[j]), gain_rows(mla_qk_norm_k[j]), gmq, gmk]
            kern = functools.partial(_pre_mla_kernel, qscale=qk ** -0.5 * LOG2E)
            widths = [nh * LANES] * 3
            ql, kl, vl = _pre_call(kern, x_lat, mod_l, [tab_l], consts, widths, "pre_mla")
            qc, kc, vc = _pre_call(kern, x_ctx, mod_c, [tab_c], consts, widths, "pre_mla_ctx")
            qc, kc, vc = as_ctx(qc), as_ctx(kc), as_ctx(vc)
            o_l = [_mla_flash_call(ql, kc, vc, kl, vl)]
            o_c = [_mla_flash_call(qc, kc, vc).reshape(1, bsz * n_ctx, -1)] if need_ctx else None
            wo = jnp.pad(mla_w_out[j].reshape(nh, MLA_V, d), ((0, 0), (0, LANES - MLA_V), (0, 0)))
            wo = wo.reshape(nh * LANES, d).astype(BF16)

        w1 = ffn_w_in[i].astype(BF16)
        w2 = ffn_w_out[i].astype(BF16)
        x_lat = _post_call(o_l, x_lat, mod_l, wo, w1, w2, gs_l, ng, name="post")
        if need_ctx:
            x_ctx = _post_call(o_c, x_ctx, mod_c, wo, w1, w2, gs_c, ng, name="post_ctx")
    return x_lat
```
